```python
import jax, jax.numpy as jnp
from jax import lax
import numpy as np

D_MODEL = 1024
BATCH = 4
SEQ = 8192
DEPTH = 2
DEC_BATCH = 128
DEC_SEQ = 4
PAST_LEN = 16384
PAGE_SIZE = 128

N_MIXERS = 2
N_LAYERS_A = (DEPTH + 1) // 2
N_LAYERS_B = DEPTH // 2
EPS = 1e-6

A_HEADS = 8
A_NOPE = 64
A_ROPE = 32
A_QK = A_NOPE + A_ROPE
A_V = 64
A_Q_LORA = 384
A_KV_LORA = 256
A_WIDTH = A_HEADS * A_V
A_IN = A_Q_LORA + A_KV_LORA + A_ROPE + A_WIDTH
ROPE_THETA = 10000.0
Q_BLOCK = 128

B_HEADS = 8
B_DK = 64
B_DV = 64
B_WIDTH = B_HEADS * B_DV
B_CONV = 4
B_CONV_CH = 2 * B_HEADS * B_DK + B_WIDTH
B_IN = B_CONV_CH + B_WIDTH + 2 * B_HEADS
CHUNK = 64

kernel_name = "hybrid_mla_gated_deltanet_step"


def rms_norm(x, g):
    xf = x.astype(jnp.float32)
    y = xf * lax.rsqrt(jnp.mean(xf * xf, -1, keepdims=True) + EPS)
    return (y * g.astype(jnp.float32)).astype(x.dtype)


def l2_norm(x):
    return x * lax.rsqrt(jnp.sum(x * x, -1, keepdims=True) + EPS)


def rope(x, pos):
    half = A_ROPE // 2
    inv = ROPE_THETA ** (-jnp.arange(half, dtype=jnp.float32) / half)
    ang = pos.astype(jnp.float32)[:, None] * inv[None, :]
    cos = jnp.cos(ang)[:, None, :]
    sin = jnp.sin(ang)[:, None, :]
    x1 = x[..., :half].astype(jnp.float32)
    x2 = x[..., half:].astype(jnp.float32)
    return jnp.concatenate([x1 * cos - x2 * sin, x2 * cos + x1 * sin], -1).astype(x.dtype)


def mla_project(x, pos, norm_g, w_in, g_qa, w_uq, g_kv, g_q):
    n, t, _ = x.shape
    h = rms_norm(x, norm_g)
    proj = h @ w_in
    o1 = A_Q_LORA
    o2 = o1 + A_KV_LORA
    o3 = o2 + A_ROPE
    q_a, c, k_pe, z = proj[..., :o1], proj[..., o1:o2], proj[..., o2:o3], proj[..., o3:]
    q = (rms_norm(q_a, g_qa) @ w_uq).reshape(n, t, A_HEADS, A_QK)
    q = jnp.concatenate([q[..., :A_NOPE], rope(q[..., A_NOPE:], pos)], -1)
    q = rms_norm(q, g_q)
    c = rms_norm(c, g_kv)
    k_pe = rope(k_pe[:, :, None, :], pos)[:, :, 0, :]
    return q, c, k_pe, z


def mla_keys(c, k_pe, w_uk, g_k):
    k_nope = jnp.einsum('...sc,chd->...shd', c, w_uk)
    k_pe_h = jnp.broadcast_to(k_pe[..., None, :], k_nope.shape[:-1] + (A_ROPE,))
    return rms_norm(jnp.concatenate([k_nope, k_pe_h], -1), g_k)


def mla_prompt_attention(q, k, v):
    b, s = q.shape[:2]
    nb = s // Q_BLOCK
    qb = q.reshape(b, nb, Q_BLOCK, A_HEADS, A_QK).transpose(1, 0, 2, 3, 4)
    kpos = jnp.arange(s)
    scale = A_QK ** -0.5

    def block(args):
        qi, i = args
        qpos = i * Q_BLOCK + jnp.arange(Q_BLOCK)
        sc = jnp.einsum('bqhd,bkhd->bhqk', qi, k).astype(jnp.float32) * scale
        sc = jnp.where(kpos[None, :] <= qpos[:, None], sc, -jnp.inf)
        p = jax.nn.softmax(sc, axis=-1).astype(v.dtype)
        return jnp.einsum('bhqk,bkhd->bqhd', p, v)

    o = lax.map(block, (qb, jnp.arange(nb)))
    return o.transpose(1, 0, 2, 3, 4).reshape(b, s, A_HEADS, A_V)


def mla_sample_attention(q, c_new, kpe_new, cache_latent, cache_krope, la, page_table, w_uk, w_uv, g_k):
    t = q.shape[1]
    past = page_table.shape[1] * PAGE_SIZE
    spos = jnp.arange(past + t)
    tpos = past + jnp.arange(t)
    mask = spos[None, :] <= tpos[:, None]
    scale = A_QK ** -0.5

    def one(args):
        qi, ci, kpi, pt = args
        c_all = jnp.concatenate([cache_latent[la, pt].reshape(past, A_KV_LORA).astype(ci.dtype), ci], 0)
        kpe_all = jnp.concatenate([cache_krope[la, pt].reshape(past, A_ROPE).astype(kpi.dtype), kpi], 0)
        k = mla_keys(c_all, kpe_all, w_uk, g_k)
        sc = jnp.einsum('thd,shd->hts', qi, k).astype(jnp.float32) * scale
        sc = jnp.where(mask[None], sc, -jnp.inf)
        p = jax.nn.softmax(sc, axis=-1).astype(c_all.dtype)
        o_lat = jnp.einsum('hts,sc->thc', p, c_all)
        return jnp.einsum('thc,chd->thd', o_lat, w_uv)

    return lax.map(one, (q, c_new, kpe_new, page_table))


def gated_out(x, o, z, w_o):
    n, t = x.shape[:2]
    return x + (o.reshape(n, t, -1) * jax.nn.silu(z)) @ w_o


def gdn_project(x, norm_g, w_in):
    h = rms_norm(x, norm_g)
    proj = h @ w_in
    o1 = B_CONV_CH
    o2 = o1 + B_WIDTH
    o3 = o2 + B_HEADS
    return proj[..., :o1], proj[..., o1:o2], proj[..., o2:o3], proj[..., o3:]


def gdn_features(ext, t, a, b, w_conv, a_log, dt_bias):
    n = ext.shape[0]
    conv = ext[:, 0:t] * w_conv[0]
    for j in range(1, B_CONV):
        conv = conv + ext[:, j:j + t] * w_conv[j]
    conv = jax.nn.silu(conv).astype(jnp.float32)
    hk = B_HEADS * B_DK
    q = l2_norm(conv[..., :hk].reshape(n, t, B_HEADS, B_DK)) * (B_DK ** -0.5)
    k = l2_norm(conv[..., hk:2 * hk].reshape(n, t, B_HEADS, B_DK))
    v = conv[..., 2 * hk:].reshape(n, t, B_HEADS, B_DV)
    g = -jnp.exp(a_log.astype(jnp.float32)) * jax.nn.softplus(a.astype(jnp.float32) + dt_bias.astype(jnp.float32))
    beta = jax.nn.sigmoid(b.astype(jnp.float32))
    return q, k, v, g, beta


def gated_delta_chunked(q, k, v, g, beta):
    b, s = q.shape[:2]
    n = s // CHUNK

    def chunks(x):
        return jnp.swapaxes(x.reshape((b, n, CHUNK) + x.shape[2:]), 2, 3)

    q, k, v, g, beta = chunks(q), chunks(k), chunks(v), chunks(g), chunks(beta)
    gc = jnp.cumsum(g, -1)
    idx = jnp.arange(CHUNK)
    strict = idx[:, None] > idx[None, :]
    incl = idx[:, None] >= idx[None, :]
    decay = jnp.exp(jnp.where(incl, gc[..., :, None] - gc[..., None, :], -jnp.inf))
    kb = k * beta[..., None]
    lmat = jnp.where(strict, jnp.einsum('...id,...jd->...ij', kb, k) * decay, 0.0)
    amat = lmat + jnp.eye(CHUNK, dtype=lmat.dtype)
    u = lax.linalg.triangular_solve(amat, v * beta[..., None], left_side=True, lower=True, unit_diagonal=True)
    w = lax.linalg.triangular_solve(amat, kb * jnp.exp(gc)[..., None], left_side=True, lower=True, unit_diagonal=True)
    intra = jnp.where(incl, jnp.einsum('...id,...jd->...ij', q, k) * decay, 0.0)

    def step(st, xs):
        qi, ki, ui, wi, gi, ai = xs
        v_new = ui - jnp.einsum('bhcd,bhde->bhce', wi, st)
        o = jnp.einsum('bhcd,bhde->bhce', qi * jnp.exp(gi)[..., None], st) + jnp.einsum('bhij,bhje->bhie', ai, v_new)
        glast = gi[..., -1]
        st = st * jnp.exp(glast)[..., None, None] + jnp.einsum(
            'bhcd,bhce->bhde', ki * jnp.exp(glast[..., None] - gi)[..., None], v_new)
        return st, o

    xs = tuple(jnp.moveaxis(t, 1, 0) for t in (q, k, u, w, gc, intra))
    st0 = jnp.zeros((b, B_HEADS, B_DK, B_DV), jnp.float32)
    st, o = lax.scan(step, st0, xs)
    o = o.transpose(1, 0, 3, 2, 4).reshape(b, s, B_HEADS, B_DV)
    return o, st


def gated_delta_recurrent(q, k, v, g, beta, st0):
    def step(st, xs):
        qt, kt, vt, gt, bt = xs
        st = st * jnp.exp(gt)[..., None, None]
        delta = (vt - jnp.einsum('nhd,nhde->nhe', kt, st)) * bt[..., None]
        st = st + jnp.einsum('nhd,nhe->nhde', kt, delta)
        return st, jnp.einsum('nhd,nhde->nhe', qt, st)

    xs = tuple(jnp.swapaxes(t, 0, 1) for t in (q, k, v, g, beta))
    st, o = lax.scan(step, st0, xs)
    return jnp.swapaxes(o, 0, 1), st


def gdn_out(x, o, z, g_o, w_o):
    n, t = x.shape[:2]
    on = rms_norm(o, g_o).astype(x.dtype)
    gated = on * jax.nn.silu(z.reshape(n, t, B_HEADS, B_DV))
    return x + gated.reshape(n, t, B_WIDTH) @ w_o


def setup_inputs(seed: int = 0) -> dict:
    key = jax.random.key(seed)
    ks = jax.random.split(key, 32)
    f32 = jnp.float32
    n_pages = PAST_LEN // PAGE_SIZE
    n_used = DEC_BATCH * n_pages
    n_pool = n_used + max(1, n_used // 4)

    def nrm(k, shape, scale):
        return jax.random.normal(k, shape, f32) * scale

    def gain(k, shape):
        return 1.0 + 0.02 * jax.random.normal(k, shape, f32)

    page_table = jax.random.permutation(ks[4], n_pool)[:n_used].reshape(DEC_BATCH, n_pages).astype(jnp.int32)
    return {
        "x_prompt": nrm(ks[0], (BATCH, SEQ, D_MODEL), 1.0),
        "x_sample": nrm(ks[1], (DEC_BATCH, DEC_SEQ, D_MODEL), 1.0),
        "cache_latent": nrm(ks[2], (N_LAYERS_A, n_pool, PAGE_SIZE, A_KV_LORA), 1.0),
        "cache_krope": nrm(ks[3], (N_LAYERS_A, n_pool, PAGE_SIZE, A_ROPE), 1.0),
        "page_table": page_table,
        "state_conv": nrm(ks[5], (N_LAYERS_B, DEC_BATCH, B_CONV - 1, B_CONV_CH), 1.0),
        "state_ssm": nrm(ks[6], (N_LAYERS_B, DEC_BATCH, B_HEADS, B_DK, B_DV), 0.3),
        "a_norm": gain(ks[7], (N_LAYERS_A, D_MODEL)),
        "a_w_in": nrm(ks[8], (N_LAYERS_A, D_MODEL, A_IN), D_MODEL ** -0.5),
        "a_g_qa": gain(ks[9], (N_LAYERS_A, A_Q_LORA)),
        "a_w_uq": nrm(ks[10], (N_LAYERS_A, A_Q_LORA, A_HEADS * A_QK), A_Q_LORA ** -0.5),
        "a_g_kv": gain(ks[11], (N_LAYERS_A, A_KV_LORA)),
        "a_w_uk": nrm(ks[12], (N_LAYERS_A, A_KV_LORA, A_HEADS, A_NOPE), A_KV_LORA ** -0.5),
        "a_w_uv": nrm(ks[13], (N_LAYERS_A, A_KV_LORA, A_HEADS, A_V), A_KV_LORA ** -0.5),
        "a_g_q": gain(ks[14], (N_LAYERS_A, A_QK)),
        "a_g_k": gain(ks[15], (N_LAYERS_A, A_QK)),
        "a_w_o": nrm(ks[16], (N_LAYERS_A, A_WIDTH, D_MODEL), A_WIDTH ** -0.5),
        "b_norm": gain(ks[17], (N_LAYERS_B, D_MODEL)),
        "b_w_in": nrm(ks[18], (N_LAYERS_B, D_MODEL, B_IN), D_MODEL ** -0.5),
        "b_w_conv": nrm(ks[19], (N_LAYERS_B, B_CONV, B_CONV_CH), B_CONV ** -0.5),
        "b_a_log": jnp.log(jax.random.uniform(ks[20], (N_LAYERS_B, B_HEADS), f32, 1.0, 16.0)),
        "b_dt_bias": nrm(ks[21], (N_LAYERS_B, B_HEADS), 0.1),
        "b_g_o": gain(ks[22], (N_LAYERS_B, B_DV)),
        "b_w_o": nrm(ks[23], (N_LAYERS_B, B_WIDTH, D_MODEL), B_WIDTH ** -0.5),
    }


def reference(x_prompt, x_sample, cache_latent, cache_krope, page_table, state_conv, state_ssm,
              a_norm, a_w_in, a_g_qa, a_w_uq, a_g_kv, a_w_uk, a_w_uv, a_g_q, a_g_k, a_w_o,
              b_norm, b_w_in, b_w_conv, b_a_log, b_dt_bias, b_g_o, b_w_o):
    past_len = page_table.shape[1] * PAGE_SIZE
    pos_p = jnp.arange(x_prompt.shape[1])
    pos_s = past_len + jnp.arange(x_sample.shape[1])
    xp, xs = x_prompt, x_sample
    lat_p, kpe_p, lat_s, kpe_s = [], [], [], []
    conv_p, ssm_p, conv_s, ssm_s = [], [], [], []
    for i in range(DEPTH):
        li = i // N_MIXERS
        if i % N_MIXERS == 0:
            wp = (a_norm[li], a_w_in[li], a_g_qa[li], a_w_uq[li], a_g_kv[li], a_g_q[li])
            q, c, kpe, z = mla_project(xp, pos_p, *wp)
            k = mla_keys(c, kpe, a_w_uk[li], a_g_k[li])
            v = jnp.einsum('bsc,chd->bshd', c, a_w_uv[li])
            o = mla_prompt_attention(q, k, v)
            xp = gated_out(xp, o, z, a_w_o[li])
            lat_p.append(c)
            kpe_p.append(kpe)
            q, c, kpe, z = mla_project(xs, pos_s, *wp)
            o = mla_sample_attention(q, c, kpe, cache_latent, cache_krope, li, page_table,
                                     a_w_uk[li], a_w_uv[li], a_g_k[li])
            xs = gated_out(xs, o, z, a_w_o[li])
            lat_s.append(c)
            kpe_s.append(kpe)
        else:
            qkv, z, a, b = gdn_project(xp, b_norm[li], b_w_in[li])
            t = xp.shape[1]
            ext = jnp.concatenate([jnp.zeros((xp.shape[0], B_CONV - 1, B_CONV_CH), qkv.dtype), qkv], 1)
            q, k, v, g, beta = gdn_features(ext, t, a, b, b_w_conv[li], b_a_log[li], b_dt_bias[li])
            o, st = gated_delta_chunked(q, k, v, g, beta)
            xp = gdn_out(xp, o, z, b_g_o[li], b_w_o[li])
            conv_p.append(ext[:, -(B_CONV - 1):])
            ssm_p.append(st.astype(xp.dtype))

            qkv, z, a, b = gdn_project(xs, b_norm[li], b_w_in[li])
            t = xs.shape[1]
            ext = jnp.concatenate([state_conv[li].astype(qkv.dtype), qkv], 1)
            q, k, v, g, beta = gdn_features(ext, t, a, b, b_w_conv[li], b_a_log[li], b_dt_bias[li])
            o, st = gated_delta_recurrent(q, k, v, g, beta, state_ssm[li].astype(jnp.float32))
            xs = gdn_out(xs, o, z, b_g_o[li], b_w_o[li])
            conv_s.append(ext[:, -(B_CONV - 1):])
            ssm_s.append(st.astype(xs.dtype))
    return (xp, xs, jnp.stack(lat_p), jnp.stack(kpe_p), jnp.stack(lat_s), jnp.stack(kpe_s),
            jnp.stack(conv_p), jnp.stack(ssm_p), jnp.stack(conv_s), jnp.stack(ssm_s))
```

```python
import functools

import jax
import jax.numpy as jnp
from jax import lax
from jax.experimental import pallas as pl
from jax.experimental.pallas import tpu as pltpu

F32 = jnp.float32
BF16 = jnp.bfloat16
EPS = 1e-6

D_MODEL = 1024
PAGE = 128
A_HEADS = 8
A_NOPE = 64
A_ROPE = 32
A_QK = A_NOPE + A_ROPE
A_V = 64
A_Q_LORA = 384
A_KV_LORA = 256
A_WIDTH = A_HEADS * A_V
ROPE_THETA = 10000.0
SLOT = 128
A_QPAD = A_HEADS * SLOT
B_HEADS = 8
B_DK = 64
B_DV = 64
B_WIDTH = B_HEADS * B_DV
B_CONV = 4
B_CONV_CH = 2 * B_HEADS * B_DK + B_WIDTH
HALO = 8

VMEM_LIMIT = 56 * 1024 * 1024


def _cparams(sem):
    return pltpu.CompilerParams(dimension_semantics=sem, vmem_limit_bytes=VMEM_LIMIT)


def _dot(a, b):
    return jnp.dot(a, b, preferred_element_type=F32)


def _dot_nt(a, b):
    return lax.dot_general(a, b, (((1,), (1,)), ((), ())), preferred_element_type=F32)


def _dot_tn(a, b):
    return lax.dot_general(a, b, (((0,), (0,)), ((), ())), preferred_element_type=F32)


def _dot_sel(sel_bf16, x, *, sel_first):
    x1 = x.astype(BF16)
    r1 = x - x1.astype(F32)
    x2 = r1.astype(BF16)
    x3 = (r1 - x2.astype(F32)).astype(BF16)
    if sel_first:
        return _dot(sel_bf16, x1) + _dot(sel_bf16, x2) + _dot(sel_bf16, x3)
    return _dot(x1, sel_bf16) + _dot(x2, sel_bf16) + _dot(x3, sel_bf16)


def _rms(x, g):
    return x * lax.rsqrt(jnp.mean(x * x, -1, keepdims=True) + EPS) * g


def _silu(x):
    return x * jax.nn.sigmoid(x)


def _softplus(x):
    return jnp.maximum(x, 0.0) + jnp.log(1.0 + jnp.exp(-jnp.abs(x)))


def _rope_table_kernel(pos_ref, inv_ref, cos_ref, sin_ref):
    ang = inv_ref[...] * pos_ref[...]
    cos_ref[...] = jnp.cos(ang)
    sin_ref[...] = jnp.sin(ang)


def _rope_tables(pos):
    half = A_ROPE // 2
    p = pos.shape[0]
    inv = (ROPE_THETA ** (-jnp.arange(half, dtype=F32) / half)).reshape(half, 1)
    cos_t, sin_t = pl.pallas_call(
        _rope_table_kernel,
        out_shape=(jax.ShapeDtypeStruct((half, p), F32),) * 2,
        name="rope_tables",
    )(pos.reshape(1, p), inv)
    cos = cos_t.T
    sin = sin_t.T
    one = jnp.ones((p, A_NOPE), F32)
    zn = jnp.zeros((p, A_NOPE), F32)
    zh = jnp.zeros((p, half), F32)
    zp = jnp.zeros((p, SLOT - A_QK), F32)
    tc = jnp.concatenate([one, cos, cos, zp], 1)
    ts1 = jnp.concatenate([zn, -sin, zh, zp], 1)
    ts2 = jnp.concatenate([zn, zh, sin, zp], 1)
    return tc, ts1, ts2


def _mla_proj_kernel(*refs, with_kv):
    (x_ref, tc_ref, ts1_ref, ts2_ref, gn_ref, win_ref, gqa_ref, wuq_ref, gkv_ref, gq_ref) = refs[:10]
    if with_kv:
        wuk_ref, gk_ref, wuv_ref = refs[10:13]
        q_out, c_out, kpe_out, z_out, k_out, v_out = refs[13:]
    else:
        q_out, c_out, kpe_out, z_out = refs[10:]
    x = x_ref[0]
    h = _rms(x, gn_ref[...])
    proj = _dot(h.astype(BF16), win_ref[...])
    o1 = A_Q_LORA
    o2 = o1 + A_KV_LORA
    o3 = o2 + A_WIDTH
    qa = _rms(proj[:, :o1], gqa_ref[...])
    c = _rms(proj[:, o1:o2], gkv_ref[...])
    z_out[0] = proj[:, o2:o3]
    c_out[0] = c
    q = _dot(qa.astype(BF16), wuq_ref[...])
    tc = tc_ref[...]
    ts1 = ts1_ref[...]
    ts2 = ts2_ref[...]

    def rope(s):
        return s * tc + pltpu.roll(s, SLOT - A_ROPE // 2, 1) * ts1 + pltpu.roll(s, A_ROPE // 2, 1) * ts2

    def head_norm(s, g):
        ms = jnp.sum(s * s, -1, keepdims=True) * (1.0 / A_QK)
        return s * lax.rsqrt(ms + EPS) * g

    kslot = rope(proj[:, o3:o3 + SLOT])
    kpe_out[0] = kslot[:, A_NOPE:A_QK]
    gq = gq_ref[...]
    for hd in range(A_HEADS):
        sl = slice(hd * SLOT, (hd + 1) * SLOT)
        q_out[0, :, sl] = head_norm(rope(q[:, sl]), gq).astype(q_out.dtype)
    if with_kv:
        cb = c.astype(BF16)
        kn = _dot(cb, wuk_ref[...])
        v_out[0] = _dot(cb, wuv_ref[...]).astype(v_out.dtype)
        gk = gk_ref[...]
        for hd in range(A_HEADS):
            sl = slice(hd * SLOT, (hd + 1) * SLOT)
            k_out[0, :, sl] = head_norm(kn[:, sl] + kslot, gk).astype(k_out.dtype)


def _mla_project(x, tabs, w, *, with_kv, tm, q_dtype):
    b, s, _ = x.shape
    tm = min(tm, s)
    grid = (s // tm, b)
    full = lambda a: pl.BlockSpec(a.shape, lambda i, j: (0,) * a.ndim)
    row = lambda n: pl.BlockSpec((1, tm, n), lambda i, j: (j, i, 0))
    tab = pl.BlockSpec((tm, SLOT), lambda i, j: (i, 0))
    params = [w["a_norm"], w["w_in"], w["g_qa"], w["w_uq"], w["g_kv"], w["g_q"]]
    out_shape = [
        jax.ShapeDtypeStruct((b, s, A_QPAD), q_dtype),
        jax.ShapeDtypeStruct((b, s, A_KV_LORA), F32),
        jax.ShapeDtypeStruct((b, s, A_ROPE), F32),
        jax.ShapeDtypeStruct((b, s, A_WIDTH), F32),
    ]
    out_specs = [row(A_QPAD), row(A_KV_LORA), row(A_ROPE), row(A_WIDTH)]
    if with_kv:
        params += [w["w_uk"], w["g_k"], w["w_uv"]]
        out_shape += [jax.ShapeDtypeStruct((b, s, A_QPAD), BF16), jax.ShapeDtypeStruct((b, s, A_WIDTH), BF16)]
        out_specs += [row(A_QPAD), row(A_WIDTH)]
    return pl.pallas_call(
        functools.partial(_mla_proj_kernel, with_kv=with_kv),
        grid=grid,
        in_specs=[row(D_MODEL), tab, tab, tab] + [full(p) for p in params],
        out_specs=out_specs,
        out_shape=out_shape,
        compiler_params=_cparams(("parallel", "parallel")),
        name="mla_proj_kv" if with_kv else "mla_proj",
    )(x, *tabs, *params)


def _flash_kernel(qi_ref, ki_ref, q_ref, k_ref, v_ref, o_ref, m_ref, l_ref, acc_ref, *, tq):
    p = pl.program_id(1)
    qi = qi_ref[p]
    ki = ki_ref[p]

    @pl.when(ki == 0)
    def _():
        m_ref[...] = jnp.full(m_ref.shape, -jnp.inf, F32)
        l_ref[...] = jnp.zeros(l_ref.shape, F32)
        acc_ref[...] = jnp.zeros(acc_ref.shape, F32)

    low = lax.broadcasted_iota(jnp.int32, (tq, SLOT), 1) < A_V

    def step(diag):
        if diag:
            keep = lax.broadcasted_iota(jnp.int32, (tq, tq), 1) <= lax.broadcasted_iota(jnp.int32, (tq, tq), 0)
        for j in range(A_HEADS // 2):
            pv = []
            al = []
            vpair = v_ref[0, :, j * SLOT:(j + 1) * SLOT]
            for e in range(2):
                hd = 2 * j + e
                sl = slice(hd * SLOT, (hd + 1) * SLOT)
                s = _dot_nt(q_ref[0, :, sl], k_ref[0, :, sl])
                if diag:
                    s = jnp.where(keep, s, -jnp.inf)
                m_prev = m_ref[hd]
                m_new = jnp.maximum(m_prev, jnp.max(s, -1, keepdims=True))
                alpha = jnp.exp(m_prev - m_new)
                pr = jnp.exp(s - m_new[:, :1])
                l_ref[hd] = alpha * l_ref[hd] + jnp.sum(pr, -1, keepdims=True)
                m_ref[hd] = m_new
                pv.append(_dot(pr.astype(BF16), vpair))
                al.append(alpha)
            sl = slice(j * SLOT, (j + 1) * SLOT)
            acc_ref[:, sl] = jnp.where(low, al[0], al[1]) * acc_ref[:, sl] + jnp.where(low, pv[0], pv[1])

    @pl.when(ki < qi)
    def _():
        step(False)

    @pl.when(ki == qi)
    def _():
        step(True)
        for j in range(A_HEADS // 2):
            sl = slice(j * SLOT, (j + 1) * SLOT)
            linv = jnp.where(low, 1.0 / l_ref[2 * j], 1.0 / l_ref[2 * j + 1])
            o_ref[0, :, sl] = acc_ref[:, sl] * linv


def _flash_attention(q, k, v, *, tq):
    b, s, _ = q.shape
    tq = min(tq, s)
    nq = s // tq
    qi = jnp.asarray([i for i in range(nq) for _ in range(i + 1)], jnp.int32)
    ki = jnp.asarray([j for i in range(nq) for j in range(i + 1)], jnp.int32)
    grid_spec = pltpu.PrefetchScalarGridSpec(
        num_scalar_prefetch=2,
        grid=(b, int(qi.shape[0])),
        in_specs=[
            pl.BlockSpec((1, tq, A_QPAD), lambda bi, p, qt, kt: (bi, qt[p], 0)),
            pl.BlockSpec((1, tq, A_QPAD), lambda bi, p, qt, kt: (bi, kt[p], 0)),
            pl.BlockSpec((1, tq, A_WIDTH), lambda bi, p, qt, kt: (bi, kt[p], 0)),
        ],
        out_specs=pl.BlockSpec((1, tq, A_WIDTH), lambda bi, p, qt, kt: (bi, qt[p], 0)),
        scratch_shapes=[
            pltpu.VMEM((A_HEADS, tq, SLOT), F32),
            pltpu.VMEM((A_HEADS, tq, SLOT), F32),
            pltpu.VMEM((tq, A_WIDTH), F32),
        ],
    )
    return pl.pallas_call(
        functools.partial(_flash_kernel, tq=tq),
        grid_spec=grid_spec,
        out_shape=jax.ShapeDtypeStruct((b, s, A_WIDTH), F32),
        compiler_params=_cparams(("parallel", "arbitrary")),
        name="mla_flash",
    )(qi, ki, q, k, v)


def _paged_kernel(*refs, ppb, n_blocks):
    pt_ref = refs[0]
    q_ref, cn_ref, kn_ref, gk_ref, wukt_ref, wuk_ref, wuv_ref, ind_ref, one_ref = refs[1:10]
    lat_refs = refs[10:10 + ppb]
    kr_refs = refs[10 + ppb:10 + 2 * ppb]
    o_ref = refs[10 + 2 * ppb]
    qabs_ref, qr_ref, m_ref, l_ref, acc_ref = refs[11 + 2 * ppb:]
    del pt_ref
    blk = pl.program_id(1)
    rows = A_HEADS * 8

    @pl.when(blk == 0)
    def _():
        m_ref[...] = jnp.full(m_ref.shape, -jnp.inf, F32)
        l_ref[...] = jnp.zeros(l_ref.shape, F32)
        acc_ref[...] = jnp.zeros(acc_ref.shape, F32)
        qabs_ref[...] = jnp.zeros(qabs_ref.shape, BF16)
        qr_ref[...] = jnp.zeros(qr_ref.shape, BF16)
        gk = gk_ref[...]
        for hd in range(A_HEADS):
            qs = q_ref[0, :, hd * SLOT:(hd + 1) * SLOT] * gk
            qabs_ref[hd * 8:(hd + 1) * 8, :] = _dot(qs[:, :A_NOPE].astype(BF16), wukt_ref[hd]).astype(BF16)
            qr_ref[hd * 8:(hd + 1) * 8, :] = qs[:, A_NOPE:A_QK].astype(BF16)

    def page(c, kr, new):
        cb = c.astype(BF16)
        krb = kr.astype(BF16)
        kn = _dot(cb, wuk_ref[...])
        ssq = _dot((kn * kn).astype(BF16), ind_ref[...]) + _dot((kr * kr).astype(BF16), one_ref[...])
        s = (_dot_nt(cb, qabs_ref[...]) + _dot_nt(krb, qr_ref[...])) * lax.rsqrt(ssq * (1.0 / A_QK) + EPS)
        st = s.T[:rows]
        if new:
            tok = lax.broadcasted_iota(jnp.int32, (rows, PAGE), 0) & 7
            key = lax.broadcasted_iota(jnp.int32, (rows, PAGE), 1)
            st = jnp.where(key <= tok, st, -jnp.inf)
        m_prev = m_ref[...]
        m_new = jnp.maximum(m_prev, jnp.max(st, -1, keepdims=True))
        alpha = jnp.exp(m_prev - m_new)
        pr = jnp.exp(st - m_new[:, :1])
        l_ref[...] = alpha * l_ref[...] + jnp.sum(pr, -1, keepdims=True)
        m_ref[...] = m_new
        acc_ref[...] = jnp.concatenate([alpha, alpha], 1) * acc_ref[...] + _dot(pr.astype(BF16), cb)

    for i in range(ppb):
        page(lat_refs[i][0, 0], kr_refs[i][0, 0], False)

    @pl.when(blk == n_blocks - 1)
    def _():
        page(cn_ref[0], kn_ref[0], True)
        o_lat = (acc_ref[...] * (1.0 / l_ref[...][:, :1])).astype(BF16)
        for hd in range(A_HEADS):
            o_ref[0, :, hd * A_V:(hd + 1) * A_V] = _dot(o_lat[hd * 8:(hd + 1) * 8], wuv_ref[hd])


def _paged_attention(q, c_new, kpe_new, cache_latent, cache_krope, la, page_table, w, *, ppb):
    n = q.shape[0]
    n_pages = page_table.shape[1]
    ppb = min(ppb, n_pages)
    n_blocks = n_pages // ppb
    full = lambda a: pl.BlockSpec(a.shape, lambda i, j, pt: (0,) * a.ndim)
    rows = A_HEADS * 8
    ind = jnp.repeat(jnp.eye(A_HEADS, dtype=BF16), A_NOPE, axis=0)
    ind = jnp.repeat(ind, 8, axis=1)
    ind = jnp.pad(ind, ((0, 0), (0, SLOT - rows)))
    one = jnp.pad(jnp.ones((A_ROPE, rows), BF16), ((0, 0), (0, SLOT - rows)))
    params = [w["g_k"], w["w_ukt"], w["w_uk_flat"], w["w_uv_h"], ind, one]

    def lat_spec(i):
        return pl.BlockSpec((1, 1, PAGE, A_KV_LORA), lambda s, j, pt, i=i: (la, pt[s, j * ppb + i], 0, 0))

    def kr_spec(i):
        return pl.BlockSpec((1, 1, PAGE, A_ROPE), lambda s, j, pt, i=i: (la, pt[s, j * ppb + i], 0, 0))

    grid_spec = pltpu.PrefetchScalarGridSpec(
        num_scalar_prefetch=1,
        grid=(n, n_blocks),
        in_specs=[
            pl.BlockSpec((1, 8, A_QPAD), lambda s, j, pt: (s, 0, 0)),
            pl.BlockSpec((1, PAGE, A_KV_LORA), lambda s, j, pt: (s, 0, 0)),
            pl.BlockSpec((1, PAGE, A_ROPE), lambda s, j, pt: (s, 0, 0)),
        ] + [full(p) for p in params] + [lat_spec(i) for i in range(ppb)] + [kr_spec(i) for i in range(ppb)],
        out_specs=pl.BlockSpec((1, 8, A_WIDTH), lambda s, j, pt: (s, 0, 0)),
        scratch_shapes=[
            pltpu.VMEM((SLOT, A_KV_LORA), BF16),
            pltpu.VMEM((SLOT, A_ROPE), BF16),
            pltpu.VMEM((rows, SLOT), F32),
            pltpu.VMEM((rows, SLOT), F32),
            pltpu.VMEM((rows, A_KV_LORA), F32),
        ],
    )
    return pl.pallas_call(
        functools.partial(_paged_kernel, ppb=ppb, n_blocks=n_blocks),
        grid_spec=grid_spec,
        out_shape=jax.ShapeDtypeStruct((n, 8, A_WIDTH), F32),
        compiler_params=_cparams(("parallel", "arbitrary")),
        name="mla_paged",
    )(page_table, q, c_new, kpe_new, *params, *([cache_latent] * ppb), *([cache_krope] * ppb))


def _gated_out_kernel(x_ref, o_ref, z_ref, w_ref, y_ref):
    gated = o_ref[...] * _silu(z_ref[...])
    y_ref[...] = x_ref[...] + _dot(gated.astype(BF16), w_ref[...])


def _gated_out(x, o, z, w_o, *, tm):
    r = x.shape[0]
    tm = min(tm, r)
    row = lambda n: pl.BlockSpec((tm, n), lambda i: (i, 0))
    return pl.pallas_call(
        _gated_out_kernel,
        grid=(r // tm,),
        in_specs=[row(D_MODEL), row(o.shape[1]), row(z.shape[1]), pl.BlockSpec(w_o.shape, lambda i: (0, 0))],
        out_specs=row(D_MODEL),
        out_shape=jax.ShapeDtypeStruct((r, D_MODEL), F32),
        compiler_params=_cparams(("parallel",)),
        name="gated_out",
    )(x, o, z, w_o)


def _gdn_proj_kernel(x_ref, gn_ref, w_ref, wabt_ref, qkv_out, z_out, ab_out, abt_out):
    hb = _rms(x_ref[...], gn_ref[...]).astype(BF16)
    proj = _dot(hb, w_ref[...])
    o1 = B_CONV_CH
    o2 = o1 + B_WIDTH
    qkv_out[...] = proj[:, :o1]
    z_out[...] = proj[:, o1:o2]
    ab_out[...] = proj[:, o2:o2 + 2 * B_HEADS]
    abt_out[...] = _dot_nt(wabt_ref[...], hb)


def _gdn_project(x, w, *, tm):
    r = x.shape[0]
    tm = min(tm, r)
    row = lambda n: pl.BlockSpec((tm, n), lambda i: (i, 0))
    full = lambda a: pl.BlockSpec(a.shape, lambda i: (0,) * a.ndim)
    params = [w["b_norm"], w["w_in"], w["w_abt"]]
    return pl.pallas_call(
        _gdn_proj_kernel,
        grid=(r // tm,),
        in_specs=[row(D_MODEL)] + [full(p) for p in params],
        out_specs=[row(B_CONV_CH), row(B_WIDTH), row(2 * B_HEADS), pl.BlockSpec((2 * B_HEADS, tm), lambda i: (0, i))],
        out_shape=[
            jax.ShapeDtypeStruct((r, B_CONV_CH), F32),
            jax.ShapeDtypeStruct((r, B_WIDTH), F32),
            jax.ShapeDtypeStruct((r, 2 * B_HEADS), F32),
            jax.ShapeDtypeStruct((2 * B_HEADS, r), F32),
        ],
        compiler_params=_cparams(("parallel",)),
        name="gdn_proj",
    )(x, *params)


def _gdn_chunk_kernel(qkv_ref, ab_ref, abt_ref, conv0_ref, st0_ref, wc_ref, alr_ref, dtr_ref, alc_ref, dtc_ref,
                      go_ref, exp_ref, tri_ref, triu_ref, o_ref, st_out, ext_ref, feat_ref, gcx_ref, bx_ref, st_ref,
                      *, t, c, valid):
    si = pl.program_id(1)
    hk = B_HEADS * B_DK

    @pl.when(si == 0)
    def _():
        ext_ref[0:HALO] = conv0_ref[0]
        st_ref[...] = st0_ref[0]

    ext_ref[HALO:HALO + t] = qkv_ref[0]
    conv = ext_ref[HALO - 3:HALO - 3 + t] * wc_ref[0:1]
    for j in range(1, B_CONV):
        conv = conv + ext_ref[HALO - 3 + j:HALO - 3 + j + t] * wc_ref[j:j + 1]
    feat_ref[...] = _silu(conv)
    ext_ref[0:HALO] = ext_ref[t:t + HALO]

    ab = ab_ref[0]
    g = -jnp.exp(alr_ref[...]) * _softplus(ab[:, :B_HEADS] + dtr_ref[...])
    beta = jax.nn.sigmoid(ab[:, B_HEADS:])
    if valid < c:
        live = (lax.broadcasted_iota(jnp.int32, (t, B_HEADS), 0) & (c - 1)) < valid
        g = jnp.where(live, g, 0.0)
        beta = jnp.where(live, beta, 0.0)
    gx = _dot_sel(exp_ref[...], g, sel_first=False)
    bx_ref[...] = _dot_sel(exp_ref[...], beta, sel_first=False)
    gcx_ref[...] = _dot_sel(tri_ref[...], gx, sel_first=True)

    ii = lax.broadcasted_iota(jnp.int32, (c, c), 0)
    jj = lax.broadcasted_iota(jnp.int32, (c, c), 1)
    incl = ii >= jj
    strict = ii > jj
    eye = (ii == jj).astype(F32)
    go = go_ref[...]

    def chunk(ci, carry):
        r0 = pl.multiple_of(ci * c, c)
        rows = pl.ds(r0, c)
        gt = -jnp.exp(alc_ref[...]) * _softplus(abt_ref[0, ci][:B_HEADS] + dtc_ref[...])
        if valid < c:
            gt = jnp.where(lax.broadcasted_iota(jnp.int32, (B_HEADS, c), 1) < valid, gt, 0.0)
        gct = _dot_sel(triu_ref[...], gt, sel_first=False)
        for hd in range(B_HEADS):
            sl = slice(hd * B_DK, (hd + 1) * B_DK)
            qh = feat_ref[rows, sl]
            kh = feat_ref[rows, hk + hd * B_DK:hk + (hd + 1) * B_DK]
            vh = feat_ref[rows, 2 * hk + hd * B_DV:2 * hk + (hd + 1) * B_DV]
            qh = qh * lax.rsqrt(jnp.sum(qh * qh, -1, keepdims=True) + EPS) * (B_DK ** -0.5)
            kh = kh * lax.rsqrt(jnp.sum(kh * kh, -1, keepdims=True) + EPS)
            gcc = gcx_ref[rows, sl]
            bc = bx_ref[rows, sl]
            gcr = jnp.broadcast_to(gct[hd:hd + 1, :], (c, c))
            decay = jnp.exp(jnp.where(incl, gcc[:, :c] - gcr, -jnp.inf))
            kb = kh * bc
            khb = kh.astype(BF16)
            lmat = jnp.where(strict, _dot_nt(kb.astype(BF16), khb) * decay, 0.0)
            intra = _dot_nt(qh.astype(BF16), khb) * decay
            inv = eye - jnp.where((ii >> 1) == (jj >> 1), lmat, 0.0)
            lg = 1
            while (1 << lg) < c:
                off = ((ii >> lg) - (jj >> lg) == 1) & (((ii >> lg) & 1) == 1)
                invb = inv.astype(BF16)
                t1 = _dot(jnp.where(off, lmat, 0.0).astype(BF16), invb)
                inv = inv - _dot(invb, t1.astype(BF16))
                lg += 1
            invb = inv.astype(BF16)
            eg = jnp.exp(gcc)
            u = _dot(invb, (vh * bc).astype(BF16))
            wmat = _dot(invb, (kb * eg).astype(BF16))
            st = st_ref[hd]
            stb = st.astype(BF16)
            v_new = u - _dot(wmat.astype(BF16), stb)
            o = _dot((qh * eg).astype(BF16), stb) + _dot(intra.astype(BF16), v_new.astype(BF16))
            glast = gcc[c - 1:c, :]
            kdec = kh * jnp.exp(glast - gcc)
            st_ref[hd] = st * jnp.exp(glast) + _dot_tn(kdec.astype(BF16), v_new.astype(BF16))
            o_ref[0, rows, sl] = o * lax.rsqrt(jnp.mean(o * o, -1, keepdims=True) + EPS) * go
        return carry

    lax.fori_loop(0, t // c, chunk, 0)

    @pl.when(si == pl.num_programs(1) - 1)
    def _():
        st_out[0] = st_ref[...]


def _gdn_chunked(qkv, ab, abt, conv0, st0, w, *, t, c, valid):
    b, s, _ = qkv.shape
    t = min(t, s)
    full = lambda a: pl.BlockSpec(a.shape, lambda i, j: (0,) * a.ndim)
    expand = jnp.repeat(jnp.eye(B_HEADS, dtype=BF16), B_DV, axis=1)
    pos = jnp.arange(t)
    tri = ((pos[:, None] >= pos[None, :]) & (pos[:, None] // c == pos[None, :] // c)).astype(BF16)
    pc = jnp.arange(c)
    triu = (pc[:, None] <= pc[None, :]).astype(BF16)
    params = [w["w_conv"], w["a_log_r"], w["dt_r"], w["a_log_c"], w["dt_c"], w["g_o"], expand, tri, triu]
    return pl.pallas_call(
        functools.partial(_gdn_chunk_kernel, t=t, c=c, valid=valid),
        grid=(b, s // t),
        in_specs=[
            pl.BlockSpec((1, t, B_CONV_CH), lambda i, j: (i, j, 0)),
            pl.BlockSpec((1, t, 2 * B_HEADS), lambda i, j: (i, j, 0)),
            pl.BlockSpec((1, t // c, 2 * B_HEADS, c), lambda i, j: (i, j, 0, 0)),
            pl.BlockSpec((1, HALO, B_CONV_CH), lambda i, j: (i, 0, 0)),
            pl.BlockSpec((1, B_HEADS, B_DK, B_DV), lambda i, j: (i, 0, 0, 0)),
        ] + [full(p) for p in params],
        out_specs=[
            pl.BlockSpec((1, t, B_WIDTH), lambda i, j: (i, j, 0)),
            pl.BlockSpec((1, B_HEADS, B_DK, B_DV), lambda i, j: (i, 0, 0, 0)),
        ],
        out_shape=[
            jax.ShapeDtypeStruct((b, s, B_WIDTH), F32),
            jax.ShapeDtypeStruct((b, B_HEADS, B_DK, B_DV), F32),
        ],
        scratch_shapes=[
            pltpu.VMEM((t + HALO, B_CONV_CH), F32),
            pltpu.VMEM((t, B_CONV_CH), F32),
            pltpu.VMEM((t, B_WIDTH), F32),
            pltpu.VMEM((t, B_WIDTH), F32),
            pltpu.VMEM((B_HEADS, B_DK, B_DV), F32),
        ],
        compiler_params=_cparams(("parallel", "arbitrary")),
        name="gdn_chunk",
    )(qkv, ab, abt, conv0, st0, *params)


def _pad_heads(wm, n_in):
    d = wm.shape[-1]
    return jnp.pad(wm, ((0, 0), (0, 0), (0, SLOT - d))).reshape(n_in, A_HEADS * SLOT)


def _mla_weights(a_norm, a_w_in, a_g_qa, a_w_uq, a_g_kv, a_w_uk, a_w_uv, a_g_q, a_g_k, a_w_o):
    o1 = A_Q_LORA
    o2 = o1 + A_KV_LORA
    o3 = o2 + A_ROPE
    kslot = jnp.pad(a_w_in[:, o2:o3], ((0, 0), (A_NOPE, SLOT - A_QK)))
    w_in = jnp.concatenate([a_w_in[:, :o2], a_w_in[:, o3:], kslot], 1).astype(BF16)
    pad_gain = lambda g: jnp.pad(g, (0, SLOT - A_QK)).reshape(1, SLOT)
    return {
        "a_norm": a_norm.reshape(1, D_MODEL),
        "w_in": w_in,
        "g_qa": a_g_qa.reshape(1, A_Q_LORA),
        "w_uq": _pad_heads(a_w_uq.reshape(A_Q_LORA, A_HEADS, A_QK), A_Q_LORA).astype(BF16),
        "g_kv": a_g_kv.reshape(1, A_KV_LORA),
        "g_q": pad_gain(a_g_q * (A_QK ** -0.5)),
        "g_k": pad_gain(a_g_k),
        "w_uk": _pad_heads(a_w_uk, A_KV_LORA).astype(BF16),
        "w_uk_flat": a_w_uk.reshape(A_KV_LORA, A_HEADS * A_NOPE).astype(BF16),
        "w_ukt": jnp.transpose(a_w_uk, (1, 2, 0)).astype(BF16),
        "w_uv": a_w_uv.reshape(A_KV_LORA, A_WIDTH).astype(BF16),
        "w_uv_h": jnp.transpose(a_w_uv, (1, 0, 2)).astype(BF16),
        "w_o": a_w_o.astype(BF16),
    }


def _gdn_weights(b_norm, b_w_in, b_w_conv, b_a_log, b_dt_bias, b_g_o, b_w_o):
    o2 = B_CONV_CH + B_WIDTH
    w_in = jnp.pad(b_w_in, ((0, 0), (0, SLOT - 2 * B_HEADS))).astype(BF16)
    return {
        "b_norm": b_norm.reshape(1, D_MODEL),
        "w_in": w_in,
        "w_abt": b_w_in[:, o2:].T.astype(BF16),
        "w_conv": b_w_conv,
        "a_log_r": b_a_log.reshape(1, B_HEADS),
        "dt_r": b_dt_bias.reshape(1, B_HEADS),
        "a_log_c": b_a_log.reshape(B_HEADS, 1),
        "dt_c": b_dt_bias.reshape(B_HEADS, 1),
        "g_o": b_g_o.reshape(1, B_DV),
        "w_o": b_w_o.astype(BF16),
    }


def _mla_layer(xp, xs, cache_latent, cache_krope, la, page_table, w, tabs_p, tabs_s):
    b, s, _ = xp.shape
    n, t, _ = xs.shape
    q, c_p, kpe_p, z, k, v = _mla_project(xp, tabs_p, w, with_kv=True, tm=256, q_dtype=BF16)
    o = _flash_attention(q, k, v, tq=512)
    yp = _gated_out(xp.reshape(b * s, D_MODEL), o.reshape(b * s, A_WIDTH), z.reshape(b * s, A_WIDTH), w["w_o"], tm=512)

    q, c_s, kpe_s, z = _mla_project(xs.reshape(1, n * t, D_MODEL), tabs_s, w, with_kv=False, tm=256, q_dtype=F32)
    c_s = c_s.reshape(n, t, A_KV_LORA)
    kpe_s = kpe_s.reshape(n, t, A_ROPE)
    q8 = jnp.pad(q.reshape(n, t, A_QPAD), ((0, 0), (0, 8 - t), (0, 0)))
    c_new = jnp.pad(c_s, ((0, 0), (0, PAGE - t), (0, 0)))
    kpe_new = jnp.pad(kpe_s, ((0, 0), (0, PAGE - t), (0, 0)))
    o = _paged_attention(q8, c_new, kpe_new, cache_latent, cache_krope, la, page_table, w, ppb=16)
    ys = _gated_out(xs.reshape(n * t, D_MODEL), o[:, :t].reshape(n * t, A_WIDTH), z.reshape(n * t, A_WIDTH),
                    w["w_o"], tm=512)
    return yp.reshape(b, s, D_MODEL), ys.reshape(n, t, D_MODEL), c_p, kpe_p, c_s, kpe_s


def _gdn_layer(xp, xs, state_conv, state_ssm, w):
    b, s, _ = xp.shape
    n, t, _ = xs.shape
    c = 64
    qkv, z, ab, abt = _gdn_project(xp.reshape(b * s, D_MODEL), w, tm=256)
    qkv = qkv.reshape(b, s, B_CONV_CH)
    abt = abt.reshape(2 * B_HEADS, b, s // c, c).transpose(1, 2, 0, 3)
    o, st_p = _gdn_chunked(qkv, ab.reshape(b, s, 2 * B_HEADS), abt,
                           jnp.zeros((b, HALO, B_CONV_CH), F32), jnp.zeros((b, B_HEADS, B_DK, B_DV), F32),
                           w, t=256, c=c, valid=c)
    yp = _gated_out(xp.reshape(b * s, D_MODEL), o.reshape(b * s, B_WIDTH), z, w["w_o"], tm=512)
    conv_p = qkv[:, s - (B_CONV - 1):]

    qkv, z, ab, abt = _gdn_project(xs.reshape(n * t, D_MODEL), w, tm=256)
    qkv = qkv.reshape(n, t, B_CONV_CH)
    pad_t = lambda a: jnp.pad(a, ((0, 0), (0, 8 - t), (0, 0)))
    abt = jnp.pad(abt.reshape(2 * B_HEADS, n, t).transpose(1, 0, 2), ((0, 0), (0, 0), (0, 8 - t)))
    conv0 = jnp.pad(state_conv, ((0, 0), (HALO - (B_CONV - 1), 0), (0, 0)))
    o, st_s = _gdn_chunked(pad_t(qkv), pad_t(ab.reshape(n, t, 2 * B_HEADS)), abt.reshape(n, 1, 2 * B_HEADS, 8),
                           conv0, state_ssm, w, t=8, c=8, valid=t)
    ys = _gated_out(xs.reshape(n * t, D_MODEL), o[:, :t].reshape(n * t, B_WIDTH), z, w["w_o"], tm=512)
    conv_s = jnp.concatenate([state_conv, qkv], 1)[:, -(B_CONV - 1):]
    return yp.reshape(b, s, D_MODEL), ys.reshape(n, t, D_MODEL), conv_p, st_p, conv_s, st_s


def kernel(x_prompt, x_sample, cache_latent, cache_krope, page_table, state_conv, state_ssm,
           a_norm, a_w_in, a_g_qa, a_w_uq, a_g_kv, a_w_uk, a_w_uv, a_g_q, a_g_k, a_w_o,
           b_norm, b_w_in, b_w_conv, b_a_log, b_dt_bias, b_g_o, b_w_o):
    s = x_prompt.shape[1]
    n, t, _ = x_sample.shape
    past = page_table.shape[1] * PAGE
    p_pad = -(-(s + t) // SLOT) * SLOT
    pos = jnp.concatenate([jnp.arange(s), past + jnp.arange(t), jnp.zeros((p_pad - s - t,), jnp.int32)]).astype(F32)
    tabs = _rope_tables(pos)
    tabs_p = tuple(tb[:s] for tb in tabs)
    tabs_s = tuple(jnp.tile(tb[s:s + t], (n, 1)) for tb in tabs)

    wa = _mla_weights(a_norm[0], a_w_in[0], a_g_qa[0], a_w_uq[0], a_g_kv[0], a_w_uk[0], a_w_uv[0], a_g_q[0],
                      a_g_k[0], a_w_o[0])
    wb = _gdn_weights(b_norm[0], b_w_in[0], b_w_conv[0], b_a_log[0], b_dt_bias[0], b_g_o[0], b_w_o[0])

    xp, xs, lat_p, kpe_p, lat_s, kpe_s = _mla_layer(x_prompt, x_sample, cache_latent, cache_krope, 0, page_table,
                                                    wa, tabs_p, tabs_s)
    xp, xs, conv_p, ssm_p, conv_s, ssm_s = _gdn_layer(xp, xs, state_conv[0], state_ssm[0], wb)
    return (xp, xs, lat_p[None], kpe_p[None], lat_s[None], kpe_s[None],
            conv_p[None], ssm_p[None], conv_s[None], ssm_s[None])
```

```python
import functools

import jax
import jax.numpy as jnp
from jax import lax
from jax.experimental import pallas as pl
from jax.experimental.pallas import tpu as pltpu

F32 = jnp.float32
BF16 = jnp.bfloat16
EPS = 1e-6

D_MODEL = 1024
PAGE = 128
A_HEADS = 8
A_NOPE = 64
A_ROPE = 32
A_QK = A_NOPE + A_ROPE
A_V = 64
A_Q_LORA = 384
A_KV_LORA = 256
A_WIDTH = A_HEADS * A_V
ROPE_THETA = 10000.0
SLOT = 128
A_QPAD = A_HEADS * SLOT
B_HEADS = 8
B_DK = 64
B_DV = 64
B_WIDTH = B_HEADS * B_DV
B_CONV = 4
B_CONV_CH = 2 * B_HEADS * B_DK + B_WIDTH
HALO = 8

VMEM_LIMIT = 56 * 1024 * 1024


def _cparams(sem):
    return pltpu.CompilerParams(dimension_semantics=sem, vmem_limit_bytes=VMEM_LIMIT)


def _dot(a, b):
    return jnp.dot(a, b, preferred_element_type=F32)


def _dot_nt(a, b):
    return lax.dot_general(a, b, (((1,), (1,)), ((), ())), preferred_element_type=F32)


def _dot_tn(a, b):
    return lax.dot_general(a, b, (((0,), (0,)), ((), ())), preferred_element_type=F32)


def _dot_sel(sel_bf16, x, *, sel_first):
    x1 = x.astype(BF16)
    r1 = x - x1.astype(F32)
    x2 = r1.astype(BF16)
    x3 = (r1 - x2.astype(F32)).astype(BF16)
    if sel_first:
        return _dot(sel_bf16, x1) + _dot(sel_bf16, x2) + _dot(sel_bf16, x3)
    return _dot(x1, sel_bf16) + _dot(x2, sel_bf16) + _dot(x3, sel_bf16)


def _rms(x, g):
    return x * lax.rsqrt(jnp.mean(x * x, -1, keepdims=True) + EPS) * g


def _silu(x):
    return x * jax.nn.sigmoid(x)


def _softplus(x):
    return jnp.maximum(x, 0.0) + jnp.log(1.0 + jnp.exp(-jnp.abs(x)))


def _rope_table_kernel(pos_ref, inv_ref, cos_ref, sin_ref):
    ang = inv_ref[...] * pos_ref[...]
    cos_ref[...] = jnp.cos(ang)
    sin_ref[...] = jnp.sin(ang)


def _rope_tables(pos):
    half = A_ROPE // 2
    p = pos.shape[0]
    inv = (ROPE_THETA ** (-jnp.arange(half, dtype=F32) / half)).reshape(half, 1)
    cos_t, sin_t = pl.pallas_call(
        _rope_table_kernel,
        out_shape=(jax.ShapeDtypeStruct((half, p), F32),) * 2,
        name="rope_tables",
    )(pos.reshape(1, p), inv)
    cos = cos_t.T
    sin = sin_t.T
    one = jnp.ones((p, A_NOPE), F32)
    zn = jnp.zeros((p, A_NOPE), F32)
    zh = jnp.zeros((p, half), F32)
    zp = jnp.zeros((p, SLOT - A_QK), F32)
    tc = jnp.concatenate([one, cos, cos, zp], 1)
    ts1 = jnp.concatenate([zn, -sin, zh, zp], 1)
    ts2 = jnp.concatenate([zn, zh, sin, zp], 1)
    return tc, ts1, ts2


def _mla_proj_kernel(*refs, with_kv):
    (x_ref, tc_ref, ts1_ref, ts2_ref, gn_ref, win_ref, gqa_ref, wuq_ref, gkv_ref, gq_ref) = refs[:10]
    if with_kv:
        wuk_ref, gk_ref, wuv_ref = refs[10:13]
        q_out, c_out, kpe_out, z_out, k_out, v_out = refs[13:]
    else:
        q_out, c_out, kpe_out, z_out = refs[10:]
    x = x_ref[0]
    h = _rms(x, gn_ref[...])
    proj = _dot(h.astype(BF16), win_ref[...])
    o1 = A_Q_LORA
    o2 = o1 + A_KV_LORA
    o3 = o2 + A_WIDTH
    qa = _rms(proj[:, :o1], gqa_ref[...])
    c = _rms(proj[:, o1:o2], gkv_ref[...])
    z_out[0] = proj[:, o2:o3]
    c_out[0] = c
    q = _dot(qa.astype(BF16), wuq_ref[...])
    tc = tc_ref[...]
    ts1 = ts1_ref[...]
    ts2 = ts2_ref[...]

    def rope(s):
        return s * tc + pltpu.roll(s, SLOT - A_ROPE // 2, 1) * ts1 + pltpu.roll(s, A_ROPE // 2, 1) * ts2

    def head_norm(s, g):
        ms = jnp.sum(s * s, -1, keepdims=True) * (1.0 / A_QK)
        return s * lax.rsqrt(ms + EPS) * g

    kslot = rope(proj[:, o3:o3 + SLOT])
    kpe_out[0] = kslot[:, A_NOPE:A_QK]
    gq = gq_ref[...]
    for hd in range(A_HEADS):
        sl = slice(hd * SLOT, (hd + 1) * SLOT)
        q_out[0, :, sl] = head_norm(rope(q[:, sl]), gq).astype(q_out.dtype)
    if with_kv:
        cb = c.astype(BF16)
        kn = _dot(cb, wuk_ref[...])
        v_out[0] = _dot(cb, wuv_ref[...]).astype(v_out.dtype)
        gk = gk_ref[...]
        for hd in range(A_HEADS):
            sl = slice(hd * SLOT, (hd + 1) * SLOT)
            k_out[0, :, sl] = head_norm(kn[:, sl] + kslot, gk).astype(k_out.dtype)


def _mla_project(x, tabs, w, *, with_kv, tm, q_dtype):
    b, s, _ = x.shape
    tm = min(tm, s)
    grid = (s // tm, b)
    full = lambda a: pl.BlockSpec(a.shape, lambda i, j: (0,) * a.ndim)
    row = lambda n: pl.BlockSpec((1, tm, n), lambda i, j: (j, i, 0))
    tab = pl.BlockSpec((tm, SLOT), lambda i, j: (i, 0))
    params = [w["a_norm"], w["w_in"], w["g_qa"], w["w_uq"], w["g_kv"], w["g_q"]]
    out_shape = [
        jax.ShapeDtypeStruct((b, s, A_QPAD), q_dtype),
        jax.ShapeDtypeStruct((b, s, A_KV_LORA), F32),
        jax.ShapeDtypeStruct((b, s, A_ROPE), F32),
        jax.ShapeDtypeStruct((b, s, A_WIDTH), F32),
    ]
    out_specs = [row(A_QPAD), row(A_KV_LORA), row(A_ROPE), row(A_WIDTH)]
    if with_kv:
        params += [w["w_uk"], w["g_k"], w["w_uv"]]
        out_shape += [jax.ShapeDtypeStruct((b, s, A_QPAD), BF16), jax.ShapeDtypeStruct((b, s, A_WIDTH), BF16)]
        out_specs += [row(A_QPAD), row(A_WIDTH)]
    return pl.pallas_call(
        functools.partial(_mla_proj_kernel, with_kv=with_kv),
        grid=grid,
        in_specs=[row(D_MODEL), tab, tab, tab] + [full(p) for p in params],
        out_specs=out_specs,
        out_shape=out_shape,
        compiler_params=_cparams(("parallel", "parallel")),
        name="mla_proj_kv" if with_kv else "mla_proj",
    )(x, *tabs, *params)


def _flash_kernel(qi_ref, ki_ref, q_ref, k_ref, v_ref, o_ref, m_ref, l_ref, acc_ref, *, tq):
    p = pl.program_id(1)
    qi = qi_ref[p]
    ki = ki_ref[p]

    @pl.when(ki == 0)
    def _():
        m_ref[...] = jnp.full(m_ref.shape, -jnp.inf, F32)
        l_ref[...] = jnp.zeros(l_ref.shape, F32)
        acc_ref[...] = jnp.zeros(acc_ref.shape, F32)

    low = lax.broadcasted_iota(jnp.int32, (tq, SLOT), 1) < A_V

    def step(diag):
        if diag:
            keep = lax.broadcasted_iota(jnp.int32, (tq, tq), 1) <= lax.broadcasted_iota(jnp.int32, (tq, tq), 0)
        for j in range(A_HEADS // 2):
            pv = []
            al = []
            vpair = v_ref[0, :, j * SLOT:(j + 1) * SLOT]
            for e in range(2):
                hd = 2 * j + e
                sl = slice(hd * SLOT, (hd + 1) * SLOT)
                s = _dot_nt(q_ref[0, :, sl], k_ref[0, :, sl])
                if diag:
                    s = jnp.where(keep, s, -jnp.inf)
                m_prev = m_ref[hd]
                m_new = jnp.maximum(m_prev, jnp.max(s, -1, keepdims=True))
                alpha = jnp.exp(m_prev - m_new)
                pr = jnp.exp(s - m_new[:, :1])
                l_ref[hd] = alpha * l_ref[hd] + jnp.sum(pr, -1, keepdims=True)
                m_ref[hd] = m_new
                pv.append(_dot(pr.astype(BF16), vpair))
                al.append(alpha)
            sl = slice(j * SLOT, (j + 1) * SLOT)
            acc_ref[:, sl] = jnp.where(low, al[0], al[1]) * acc_ref[:, sl] + jnp.where(low, pv[0], pv[1])

    @pl.when(ki < qi)
    def _():
        step(False)

    @pl.when(ki == qi)
    def _():
        step(True)
        for j in range(A_HEADS // 2):
            sl = slice(j * SLOT, (j + 1) * SLOT)
            linv = jnp.where(low, 1.0 / l_ref[2 * j], 1.0 / l_ref[2 * j + 1])
            o_ref[0, :, sl] = acc_ref[:, sl] * linv


def _flash_attention(q, k, v, *, tq):
    b, s, _ = q.shape
    tq = min(tq, s)
    nq = s // tq
    qi = jnp.asarray([i for i in range(nq) for _ in range(i + 1)], jnp.int32)
    ki = jnp.asarray([j for i in range(nq) for j in range(i + 1)], jnp.int32)
    grid_spec = pltpu.PrefetchScalarGridSpec(
        num_scalar_prefetch=2,
        grid=(b, int(qi.shape[0])),
        in_specs=[
            pl.BlockSpec((1, tq, A_QPAD), lambda bi, p, qt, kt: (bi, qt[p], 0)),
            pl.BlockSpec((1, tq, A_QPAD), lambda bi, p, qt, kt: (bi, kt[p], 0)),
            pl.BlockSpec((1, tq, A_WIDTH), lambda bi, p, qt, kt: (bi, kt[p], 0)),
        ],
        out_specs=pl.BlockSpec((1, tq, A_WIDTH), lambda bi, p, qt, kt: (bi, qt[p], 0)),
        scratch_shapes=[
            pltpu.VMEM((A_HEADS, tq, SLOT), F32),
            pltpu.VMEM((A_HEADS, tq, SLOT), F32),
            pltpu.VMEM((tq, A_WIDTH), F32),
        ],
    )
    return pl.pallas_call(
        functools.partial(_flash_kernel, tq=tq),
        grid_spec=grid_spec,
        out_shape=jax.ShapeDtypeStruct((b, s, A_WIDTH), F32),
        compiler_params=_cparams(("parallel", "arbitrary")),
        name="mla_flash",
    )(qi, ki, q, k, v)


def _paged_kernel(*refs, ppb, n_blocks):
    pt_ref = refs[0]
    q_ref, cn_ref, kn_ref, gk_ref, wukt_ref, wuk_ref, wuv_ref, ind_ref = refs[1:9]
    lat_refs = refs[9:9 + ppb]
    kr_refs = refs[9 + ppb:9 + 2 * ppb]
    o_ref = refs[9 + 2 * ppb]
    qabs_ref, qsel_ref, m_ref, l_ref, acc_ref = refs[10 + 2 * ppb:]
    del pt_ref
    blk = pl.program_id(1)
    rows = A_HEADS * 8

    @pl.when(blk == 0)
    def _():
        m_ref[...] = jnp.full(m_ref.shape, -jnp.inf, F32)
        l_ref[...] = jnp.zeros(l_ref.shape, F32)
        acc_ref[...] = jnp.zeros(acc_ref.shape, F32)
        qabs_ref[...] = jnp.zeros(qabs_ref.shape, BF16)
        col = lax.broadcasted_iota(jnp.int32, (rows, 2 * A_ROPE), 1)
        qsel_ref[rows:2 * rows, :] = jnp.where(col >= A_ROPE, 1.0, 0.0).astype(BF16)
        gk = gk_ref[...]
        for hd in range(A_HEADS):
            qs = q_ref[0, :, hd * SLOT:(hd + 1) * SLOT] * gk
            qabs_ref[hd * 8:(hd + 1) * 8, :] = _dot(qs[:, :A_NOPE].astype(BF16), wukt_ref[hd]).astype(BF16)
            qsel_ref[hd * 8:(hd + 1) * 8, :] = jnp.concatenate(
                [qs[:, A_NOPE:A_QK], jnp.zeros((8, A_ROPE), F32)], 1).astype(BF16)

    def scores(cb, krt):
        kn = _dot(cb, wuk_ref[...])
        ssq_c = _dot((kn * kn).astype(BF16), ind_ref[...])
        s_c = _dot_nt(cb, qabs_ref[...])
        sk = _dot(qsel_ref[...], jnp.concatenate([krt, krt * krt], 0).astype(BF16))
        ssq = ssq_c.T[:rows] + sk[rows:]
        return (s_c.T[:rows] + sk[:rows]) * lax.rsqrt(ssq * (1.0 / A_QK) + EPS)

    def block(cb, krt, new):
        st = scores(cb, krt)
        nk = cb.shape[0]
        if new:
            tok = lax.broadcasted_iota(jnp.int32, (rows, nk), 0) & 7
            key = lax.broadcasted_iota(jnp.int32, (rows, nk), 1)
            st = jnp.where(key <= tok, st, -jnp.inf)
        m_prev = m_ref[...]
        m_new = jnp.maximum(m_prev, jnp.max(st, -1, keepdims=True))
        alpha = jnp.exp(m_prev - m_new)
        pr = jnp.exp(st - m_new[:, :1])
        l_ref[...] = alpha * l_ref[...] + jnp.sum(pr, -1, keepdims=True)
        m_ref[...] = m_new
        acc_ref[...] = jnp.concatenate([alpha, alpha], 1) * acc_ref[...] + _dot(pr.astype(BF16), cb)

    block(jnp.concatenate([r[0, 0].astype(BF16) for r in lat_refs], 0),
          jnp.concatenate([r[0, 0] for r in kr_refs], 1), False)

    @pl.when(blk == n_blocks - 1)
    def _():
        block(cn_ref[0].astype(BF16), kn_ref[0], True)
        o_lat = (acc_ref[...] * (1.0 / l_ref[...][:, :1])).astype(BF16)
        for hd in range(A_HEADS):
            o_ref[0, :, hd * A_V:(hd + 1) * A_V] = _dot(o_lat[hd * 8:(hd + 1) * 8], wuv_ref[hd])


def _paged_attention(q, c_new, kpe_new_t, cache_latent, cache_krope_t, la, page_table, w, *, ppb):
    n = q.shape[0]
    n_pages = page_table.shape[1]
    ppb = min(ppb, n_pages)
    n_blocks = n_pages // ppb
    full = lambda a: pl.BlockSpec(a.shape, lambda i, j, pt: (0,) * a.ndim)
    rows = A_HEADS * 8
    ind = jnp.repeat(jnp.eye(A_HEADS, dtype=BF16), A_NOPE, axis=0)
    ind = jnp.repeat(ind, 8, axis=1)
    ind = jnp.pad(ind, ((0, 0), (0, SLOT - rows)))
    params = [w["g_k"], w["w_ukt"], w["w_uk_flat"], w["w_uv_h"], ind]

    def lat_spec(i):
        return pl.BlockSpec((1, 1, PAGE, A_KV_LORA), lambda s, j, pt, i=i: (la, pt[s, j * ppb + i], 0, 0))

    def kr_spec(i):
        return pl.BlockSpec((1, 1, A_ROPE, PAGE), lambda s, j, pt, i=i: (la, pt[s, j * ppb + i], 0, 0))

    grid_spec = pltpu.PrefetchScalarGridSpec(
        num_scalar_prefetch=1,
        grid=(n, n_blocks),
        in_specs=[
            pl.BlockSpec((1, 8, A_QPAD), lambda s, j, pt: (s, 0, 0)),
            pl.BlockSpec((1, PAGE, A_KV_LORA), lambda s, j, pt: (s, 0, 0)),
            pl.BlockSpec((1, A_ROPE, PAGE), lambda s, j, pt: (s, 0, 0)),
        ] + [full(p) for p in params] + [lat_spec(i) for i in range(ppb)] + [kr_spec(i) for i in range(ppb)],
        out_specs=pl.BlockSpec((1, 8, A_WIDTH), lambda s, j, pt: (s, 0, 0)),
        scratch_shapes=[
            pltpu.VMEM((SLOT, A_KV_LORA), BF16),
            pltpu.VMEM((2 * rows, 2 * A_ROPE), BF16),
            pltpu.VMEM((rows, SLOT), F32),
            pltpu.VMEM((rows, SLOT), F32),
            pltpu.VMEM((rows, A_KV_LORA), F32),
        ],
    )
    return pl.pallas_call(
        functools.partial(_paged_kernel, ppb=ppb, n_blocks=n_blocks),
        grid_spec=grid_spec,
        out_shape=jax.ShapeDtypeStruct((n, 8, A_WIDTH), F32),
        compiler_params=_cparams(("parallel", "arbitrary")),
        name="mla_paged",
    )(page_table, q, c_new, kpe_new_t, *params, *([cache_latent] * ppb), *([cache_krope_t] * ppb))


def _gated_out_kernel(x_ref, o_ref, z_ref, w_ref, y_ref):
    gated = o_ref[...] * _silu(z_ref[...])
    y_ref[...] = x_ref[...] + _dot(gated.astype(BF16), w_ref[...])


def _gated_out(x, o, z, w_o, *, tm):
    r = x.shape[0]
    tm = min(tm, r)
    row = lambda n: pl.BlockSpec((tm, n), lambda i: (i, 0))
    return pl.pallas_call(
        _gated_out_kernel,
        grid=(r // tm,),
        in_specs=[row(D_MODEL), row(o.shape[1]), row(z.shape[1]), pl.BlockSpec(w_o.shape, lambda i: (0, 0))],
        out_specs=row(D_MODEL),
        out_shape=jax.ShapeDtypeStruct((r, D_MODEL), F32),
        compiler_params=_cparams(("parallel",)),
        name="gated_out",
    )(x, o, z, w_o)


def _gdn_proj_kernel(x_ref, gn_ref, w_ref, wabt_ref, qkv_out, z_out, ab_out, abt_out):
    hb = _rms(x_ref[...], gn_ref[...]).astype(BF16)
    proj = _dot(hb, w_ref[...])
    o1 = B_CONV_CH
    o2 = o1 + B_WIDTH
    qkv_out[...] = proj[:, :o1]
    z_out[...] = proj[:, o1:o2]
    ab_out[...] = proj[:, o2:o2 + 2 * B_HEADS]
    abt_out[...] = _dot_nt(wabt_ref[...], hb)


def _gdn_project(x, w, *, tm):
    r = x.shape[0]
    tm = min(tm, r)
    row = lambda n: pl.BlockSpec((tm, n), lambda i: (i, 0))
    full = lambda a: pl.BlockSpec(a.shape, lambda i: (0,) * a.ndim)
    params = [w["b_norm"], w["w_in"], w["w_abt"]]
    return pl.pallas_call(
        _gdn_proj_kernel,
        grid=(r // tm,),
        in_specs=[row(D_MODEL)] + [full(p) for p in params],
        out_specs=[row(B_CONV_CH), row(B_WIDTH), row(2 * B_HEADS), pl.BlockSpec((2 * B_HEADS, tm), lambda i: (0, i))],
        out_shape=[
            jax.ShapeDtypeStruct((r, B_CONV_CH), F32),
            jax.ShapeDtypeStruct((r, B_WIDTH), F32),
            jax.ShapeDtypeStruct((r, 2 * B_HEADS), F32),
            jax.ShapeDtypeStruct((2 * B_HEADS, r), F32),
        ],
        compiler_params=_cparams(("parallel",)),
        name="gdn_proj",
    )(x, *params)


def _gdn_chunk_kernel(qkv_ref, ab_ref, abt_ref, conv0_ref, st0_ref, wc_ref, alr_ref, dtr_ref, alc_ref, dtc_ref,
                      go_ref, exp_ref, tri_ref, triu_ref, o_ref, st_out, ext_ref, feat_ref, gcx_ref, bx_ref, st_ref,
                      lm_ref, inv_ref, t1_ref, in_ref, *, t, valid):
    si = pl.program_id(1)
    hk = B_HEADS * B_DK
    heads = range(B_HEADS)
    qsl = lambda hd: slice(hd * B_DK, (hd + 1) * B_DK)
    ksl = lambda hd: slice(hk + hd * B_DK, hk + (hd + 1) * B_DK)
    vsl = lambda hd: slice(2 * hk + hd * B_DV, 2 * hk + (hd + 1) * B_DV)

    @pl.when(si == 0)
    def _():
        ext_ref[0:HALO] = conv0_ref[0]
        st_ref[...] = st0_ref[0]

    ext_ref[HALO:HALO + t] = qkv_ref[0]
    conv = ext_ref[HALO - 3:HALO - 3 + t] * wc_ref[0:1]
    for j in range(1, B_CONV):
        conv = conv + ext_ref[HALO - 3 + j:HALO - 3 + j + t] * wc_ref[j:j + 1]
    feat_ref[...] = _silu(conv)
    ext_ref[0:HALO] = ext_ref[t:t + HALO]

    ab = ab_ref[0]
    g = -jnp.exp(alr_ref[...]) * _softplus(ab[:, :B_HEADS] + dtr_ref[...])
    beta = jax.nn.sigmoid(ab[:, B_HEADS:])
    if valid < t:
        live = lax.broadcasted_iota(jnp.int32, (t, B_HEADS), 0) < valid
        g = jnp.where(live, g, 0.0)
        beta = jnp.where(live, beta, 0.0)
    bx_ref[...] = _dot_sel(exp_ref[...], beta, sel_first=False)
    gcx_ref[...] = _dot_sel(tri_ref[...], _dot_sel(exp_ref[...], g, sel_first=False), sel_first=True)
    gt = -jnp.exp(alc_ref[...]) * _softplus(abt_ref[0, 0][:B_HEADS] + dtc_ref[...])
    if valid < t:
        gt = jnp.where(lax.broadcasted_iota(jnp.int32, (B_HEADS, t), 1) < valid, gt, 0.0)
    gct = _dot_sel(triu_ref[...], gt, sel_first=False)

    ii = lax.broadcasted_iota(jnp.int32, (t, t), 0)
    jj = lax.broadcasted_iota(jnp.int32, (t, t), 1)
    incl = ii >= jj
    strict = ii > jj
    eye = (ii == jj).astype(F32)
    pair = (ii >> 1) == (jj >> 1)

    def cols(x):
        return x[:, :t] if t <= B_DK else jnp.tile(x, (1, t // B_DK))

    for hd in heads:
        qh = feat_ref[:, qsl(hd)]
        kh = feat_ref[:, ksl(hd)]
        qh = qh * lax.rsqrt(jnp.sum(qh * qh, -1, keepdims=True) + EPS) * (B_DK ** -0.5)
        kh = kh * lax.rsqrt(jnp.sum(kh * kh, -1, keepdims=True) + EPS)
        feat_ref[:, qsl(hd)] = qh
        feat_ref[:, ksl(hd)] = kh
        decay = jnp.exp(jnp.where(incl, cols(gcx_ref[:, qsl(hd)]) - gct[hd:hd + 1, :], -jnp.inf))
        khb = kh.astype(BF16)
        lmat = jnp.where(strict, _dot_nt((kh * bx_ref[:, qsl(hd)]).astype(BF16), khb) * decay, 0.0)
        lm_ref[hd] = lmat
        in_ref[hd] = (_dot_nt(qh.astype(BF16), khb) * decay).astype(BF16)
        inv_ref[hd] = eye - jnp.where(pair, lmat, 0.0)

    lg = 1
    while (1 << lg) < t:
        off = ((ii >> lg) - (jj >> lg) == 1) & (((ii >> lg) & 1) == 1)
        for hd in heads:
            t1_ref[hd] = _dot(jnp.where(off, lm_ref[hd], 0.0).astype(BF16), inv_ref[hd].astype(BF16)).astype(BF16)
        for hd in heads:
            inv_ref[hd] = inv_ref[hd] - _dot(inv_ref[hd].astype(BF16), t1_ref[hd])
        lg += 1

    go = go_ref[...]
    us, ws, egs = [], [], []
    for hd in heads:
        invb = inv_ref[hd].astype(BF16)
        bc = bx_ref[:, qsl(hd)]
        eg = jnp.exp(gcx_ref[:, qsl(hd)])
        kb = feat_ref[:, ksl(hd)] * bc
        us.append(_dot(invb, (feat_ref[:, vsl(hd)] * bc).astype(BF16)))
        ws.append(_dot(invb, (kb * eg).astype(BF16)).astype(BF16))
        egs.append(eg)
    vns = []
    for hd in heads:
        vns.append((us[hd] - _dot(ws[hd], st_ref[hd].astype(BF16))).astype(BF16))
    for hd in heads:
        st = st_ref[hd]
        o = _dot((feat_ref[:, qsl(hd)] * egs[hd]).astype(BF16), st.astype(BF16)) + _dot(in_ref[hd], vns[hd])
        gcc = gcx_ref[:, qsl(hd)]
        glast = gcc[t - 1:t, :]
        kdec = feat_ref[:, ksl(hd)] * jnp.exp(glast - gcc)
        st_ref[hd] = st * jnp.exp(glast) + _dot_tn(kdec.astype(BF16), vns[hd])
        o_ref[0, :, qsl(hd)] = o * lax.rsqrt(jnp.mean(o * o, -1, keepdims=True) + EPS) * go

    @pl.when(si == pl.num_programs(1) - 1)
    def _():
        st_out[0] = st_ref[...]


def _gdn_chunked(qkv, ab, abt, conv0, st0, w, *, t, valid):
    b, s, _ = qkv.shape
    t = min(t, s)
    full = lambda a: pl.BlockSpec(a.shape, lambda i, j: (0,) * a.ndim)
    expand = jnp.repeat(jnp.eye(B_HEADS, dtype=BF16), B_DV, axis=1)
    pos = jnp.arange(t)
    tri = (pos[:, None] >= pos[None, :]).astype(BF16)
    triu = (pos[:, None] <= pos[None, :]).astype(BF16)
    params = [w["w_conv"], w["a_log_r"], w["dt_r"], w["a_log_c"], w["dt_c"], w["g_o"], expand, tri, triu]
    return pl.pallas_call(
        functools.partial(_gdn_chunk_kernel, t=t, valid=valid),
        grid=(b, s // t),
        in_specs=[
            pl.BlockSpec((1, t, B_CONV_CH), lambda i, j: (i, j, 0)),
            pl.BlockSpec((1, t, 2 * B_HEADS), lambda i, j: (i, j, 0)),
            pl.BlockSpec((1, 1, 2 * B_HEADS, t), lambda i, j: (i, j, 0, 0)),
            pl.BlockSpec((1, HALO, B_CONV_CH), lambda i, j: (i, 0, 0)),
            pl.BlockSpec((1, B_HEADS, B_DK, B_DV), lambda i, j: (i, 0, 0, 0)),
        ] + [full(p) for p in params],
        out_specs=[
            pl.BlockSpec((1, t, B_WIDTH), lambda i, j: (i, j, 0)),
            pl.BlockSpec((1, B_HEADS, B_DK, B_DV), lambda i, j: (i, 0, 0, 0)),
        ],
        out_shape=[
            jax.ShapeDtypeStruct((b, s, B_WIDTH), F32),
            jax.ShapeDtypeStruct((b, B_HEADS, B_DK, B_DV), F32),
        ],
        scratch_shapes=[
            pltpu.VMEM((t + HALO, B_CONV_CH), F32),
            pltpu.VMEM((t, B_CONV_CH), F32),
            pltpu.VMEM((t, B_WIDTH), F32),
            pltpu.VMEM((t, B_WIDTH), F32),
            pltpu.VMEM((B_HEADS, B_DK, B_DV), F32),
            pltpu.VMEM((B_HEADS, t, t), F32),
            pltpu.VMEM((B_HEADS, t, t), F32),
            pltpu.VMEM((B_HEADS, t, t), BF16),
            pltpu.VMEM((B_HEADS, t, t), BF16),
        ],
        compiler_params=_cparams(("parallel", "arbitrary")),
        name="gdn_chunk",
    )(qkv, ab, abt, conv0, st0, *params)


def _pad_heads(wm, n_in):
    d = wm.shape[-1]
    return jnp.pad(wm, ((0, 0), (0, 0), (0, SLOT - d))).reshape(n_in, A_HEADS * SLOT)


def _mla_weights(a_norm, a_w_in, a_g_qa, a_w_uq, a_g_kv, a_w_uk, a_w_uv, a_g_q, a_g_k, a_w_o):
    o1 = A_Q_LORA
    o2 = o1 + A_KV_LORA
    o3 = o2 + A_ROPE
    kslot = jnp.pad(a_w_in[:, o2:o3], ((0, 0), (A_NOPE, SLOT - A_QK)))
    w_in = jnp.concatenate([a_w_in[:, :o2], a_w_in[:, o3:], kslot], 1).astype(BF16)
    pad_gain = lambda g: jnp.pad(g, (0, SLOT - A_QK)).reshape(1, SLOT)
    return {
        "a_norm": a_norm.reshape(1, D_MODEL),
        "w_in": w_in,
        "g_qa": a_g_qa.reshape(1, A_Q_LORA),
        "w_uq": _pad_heads(a_w_uq.reshape(A_Q_LORA, A_HEADS, A_QK), A_Q_LORA).astype(BF16),
        "g_kv": a_g_kv.reshape(1, A_KV_LORA),
        "g_q": pad_gain(a_g_q * (A_QK ** -0.5)),
        "g_k": pad_gain(a_g_k),
        "w_uk": _pad_heads(a_w_uk, A_KV_LORA).astype(BF16),
        "w_uk_flat": a_w_uk.reshape(A_KV_LORA, A_HEADS * A_NOPE).astype(BF16),
        "w_ukt": jnp.transpose(a_w_uk, (1, 2, 0)).astype(BF16),
        "w_uv": a_w_uv.reshape(A_KV_LORA, A_WIDTH).astype(BF16),
        "w_uv_h": jnp.transpose(a_w_uv, (1, 0, 2)).astype(BF16),
        "w_o": a_w_o.astype(BF16),
    }


def _gdn_weights(b_norm, b_w_in, b_w_conv, b_a_log, b_dt_bias, b_g_o, b_w_o):
    o2 = B_CONV_CH + B_WIDTH
    w_in = jnp.pad(b_w_in, ((0, 0), (0, SLOT - 2 * B_HEADS))).astype(BF16)
    return {
        "b_norm": b_norm.reshape(1, D_MODEL),
        "w_in": w_in,
        "w_abt": b_w_in[:, o2:].T.astype(BF16),
        "w_conv": b_w_conv,
        "a_log_r": b_a_log.reshape(1, B_HEADS),
        "dt_r": b_dt_bias.reshape(1, B_HEADS),
        "a_log_c": b_a_log.reshape(B_HEADS, 1),
        "dt_c": b_dt_bias.reshape(B_HEADS, 1),
        "g_o": b_g_o.reshape(1, B_DV),
        "w_o": b_w_o.astype(BF16),
    }


def _mla_layer(xp, xs, cache_latent, cache_krope, la, page_table, w, tabs_p, tabs_s):
    b, s, _ = xp.shape
    n, t, _ = xs.shape
    q, c_p, kpe_p, z, k, v = _mla_project(xp, tabs_p, w, with_kv=True, tm=256, q_dtype=BF16)
    o = _flash_attention(q, k, v, tq=512)
    yp = _gated_out(xp.reshape(b * s, D_MODEL), o.reshape(b * s, A_WIDTH), z.reshape(b * s, A_WIDTH), w["w_o"], tm=512)

    q, c_s, kpe_s, z = _mla_project(xs.reshape(1, n * t, D_MODEL), tabs_s, w, with_kv=False, tm=256, q_dtype=F32)
    c_s = c_s.reshape(n, t, A_KV_LORA)
    kpe_s = kpe_s.reshape(n, t, A_ROPE)
    q8 = jnp.pad(q.reshape(n, t, A_QPAD), ((0, 0), (0, 8 - t), (0, 0)))
    c_new = jnp.pad(c_s, ((0, 0), (0, PAGE - t), (0, 0)))
    kpe_new_t = jnp.pad(jnp.swapaxes(kpe_s, 1, 2), ((0, 0), (0, 0), (0, PAGE - t)))
    o = _paged_attention(q8, c_new, kpe_new_t, cache_latent, jnp.swapaxes(cache_krope, 2, 3), la, page_table, w,
                         ppb=16)
    ys = _gated_out(xs.reshape(n * t, D_MODEL), o[:, :t].reshape(n * t, A_WIDTH), z.reshape(n * t, A_WIDTH),
                    w["w_o"], tm=512)
    return yp.reshape(b, s, D_MODEL), ys.reshape(n, t, D_MODEL), c_p, kpe_p, c_s, kpe_s


def _gdn_layer(xp, xs, state_conv, state_ssm, w):
    b, s, _ = xp.shape
    n, t, _ = xs.shape
    tc = min(256, s)
    qkv, z, ab, abt = _gdn_project(xp.reshape(b * s, D_MODEL), w, tm=256)
    qkv = qkv.reshape(b, s, B_CONV_CH)
    abt = abt.reshape(2 * B_HEADS, b, s // tc, tc).transpose(1, 2, 0, 3)
    o, st_p = _gdn_chunked(qkv, ab.reshape(b, s, 2 * B_HEADS), abt,
                           jnp.zeros((b, HALO, B_CONV_CH), F32), jnp.zeros((b, B_HEADS, B_DK, B_DV), F32),
                           w, t=tc, valid=tc)
    yp = _gated_out(xp.reshape(b * s, D_MODEL), o.reshape(b * s, B_WIDTH), z, w["w_o"], tm=512)
    conv_p = qkv[:, s - (B_CONV - 1):]

    qkv, z, ab, abt = _gdn_project(xs.reshape(n * t, D_MODEL), w, tm=256)
    qkv = qkv.reshape(n, t, B_CONV_CH)
    pad_t = lambda a: jnp.pad(a, ((0, 0), (0, 8 - t), (0, 0)))
    abt = jnp.pad(abt.reshape(2 * B_HEADS, n, t).transpose(1, 0, 2), ((0, 0), (0, 0), (0, 8 - t)))
    conv0 = jnp.pad(state_conv, ((0, 0), (HALO - (B_CONV - 1), 0), (0, 0)))
    o, st_s = _gdn_chunked(pad_t(qkv), pad_t(ab.reshape(n, t, 2 * B_HEADS)), abt.reshape(n, 1, 2 * B_HEADS, 8),
                           conv0, state_ssm, w, t=8, valid=t)
    ys = _gated_out(xs.reshape(n * t, D_MODEL), o[:, :t].reshape(n * t, B_WIDTH), z, w["w_o"], tm=512)
    conv_s = jnp.concatenate([state_conv, qkv], 1)[:, -(B_CONV - 1):]
    return yp.reshape(b, s, D_MODEL), ys.reshape(n, t, D_MODEL), conv_p, st_p, conv_s, st_s


def kernel(x_prompt, x_sample, cache_latent, cache_krope, page_table, state_conv, state_ssm,
           a_norm, a_w_in, a_g_qa, a_w_uq, a_g_kv, a_w_uk, a_w_uv, a_g_q, a_g_k, a_w_o,
           b_norm, b_w_in, b_w_conv, b_a_log, b_dt_bias, b_g_o, b_w_o):
    s = x_prompt.shape[1]
    n, t, _ = x_sample.shape
    past = page_table.shape[1] * PAGE
    p_pad = -(-(s + t) // SLOT) * SLOT
    pos = jnp.concatenate([jnp.arange(s), past + jnp.arange(t), jnp.zeros((p_pad - s - t,), jnp.int32)]).astype(F32)
    tabs = _rope_tables(pos)
    tabs_p = tuple(tb[:s] for tb in tabs)
    tabs_s = tuple(jnp.tile(tb[s:s + t], (n, 1)) for tb in tabs)

    wa = _mla_weights(a_norm[0], a_w_in[0], a_g_qa[0], a_w_uq[0], a_g_kv[0], a_w_uk[0], a_w_uv[0], a_g_q[0],
                      a_g_k[0], a_w_o[0])
    wb = _gdn_weights(b_norm[0], b_w_in[0], b_w_conv[0], b_a_log[0], b_dt_bias[0], b_g_o[0], b_w_o[0])

    xp, xs, lat_p, kpe_p, lat_s, kpe_s = _mla_layer(x_prompt, x_sample, cache_latent, cache_krope, 0, page_table,
                                                    wa, tabs_p, tabs_s)
    xp, xs, conv_p, ssm_p, conv_s, ssm_s = _gdn_layer(xp, xs, state_conv[0], state_ssm[0], wb)
    return (xp, xs, lat_p[None], kpe_p[None], lat_s[None], kpe_s[None],
            conv_p[None], ssm_p[None], conv_s[None], ssm_s[None])
```

```python
import functools

import jax
import jax.numpy as jnp
from jax import lax
from jax.experimental import pallas as pl
from jax.experimental.pallas import tpu as pltpu

F32 = jnp.float32
BF16 = jnp.bfloat16
EPS = 1e-6

D_MODEL = 1024
PAGE = 128
A_HEADS = 8
A_NOPE = 64
A_ROPE = 32
A_QK = A_NOPE + A_ROPE
A_V = 64
A_Q_LORA = 384
A_KV_LORA = 256
A_WIDTH = A_HEADS * A_V
ROPE_THETA = 10000.0
SLOT = 128
A_QPAD = A_HEADS * SLOT
B_HEADS = 8
B_DK = 64
B_DV = 64
B_WIDTH = B_HEADS * B_DV
B_CONV = 4
B_CONV_CH = 2 * B_HEADS * B_DK + B_WIDTH
HALO = 8

VMEM_LIMIT = 56 * 1024 * 1024
FIXED_SHIFT_MAX = 30.0
BF16_NORM_SLACK = 1.02
PAGED_PARTS = 2


def _cparams(sem):
    return pltpu.CompilerParams(dimension_semantics=sem, vmem_limit_bytes=VMEM_LIMIT)


def _dot(a, b):
    return jnp.dot(a, b, preferred_element_type=F32)


def _dot_nt(a, b):
    return lax.dot_general(a, b, (((1,), (1,)), ((), ())), preferred_element_type=F32)


def _dot_tn(a, b):
    return lax.dot_general(a, b, (((0,), (0,)), ((), ())), preferred_element_type=F32)


def _dot_sel(sel_bf16, x, *, sel_first):
    x1 = x.astype(BF16)
    r1 = x - x1.astype(F32)
    x2 = r1.astype(BF16)
    x3 = (r1 - x2.astype(F32)).astype(BF16)
    if sel_first:
        return _dot(sel_bf16, x1) + _dot(sel_bf16, x2) + _dot(sel_bf16, x3)
    return _dot(x1, sel_bf16) + _dot(x2, sel_bf16) + _dot(x3, sel_bf16)


def _rms(x, g):
    return x * lax.rsqrt(jnp.mean(x * x, -1, keepdims=True) + EPS) * g


def _silu(x):
    return x * jax.nn.sigmoid(x)


def _softplus(x):
    return jnp.maximum(x, 0.0) + jnp.log(1.0 + jnp.exp(-jnp.abs(x)))


def _rope_table_kernel(pos_ref, inv_ref, cos_ref, sin_ref):
    ang = inv_ref[...] * pos_ref[...]
    cos_ref[...] = jnp.cos(ang)
    sin_ref[...] = jnp.sin(ang)


def _rope_tables(pos):
    half = A_ROPE // 2
    p = pos.shape[0]
    inv = (ROPE_THETA ** (-jnp.arange(half, dtype=F32) / half)).reshape(half, 1)
    cos_t, sin_t = pl.pallas_call(
        _rope_table_kernel,
        out_shape=(jax.ShapeDtypeStruct((half, p), F32),) * 2,
        name="rope_tables",
    )(pos.reshape(1, p), inv)
    cos = cos_t.T
    sin = sin_t.T
    one = jnp.ones((p, A_NOPE), F32)
    zn = jnp.zeros((p, A_NOPE), F32)
    zh = jnp.zeros((p, half), F32)
    zp = jnp.zeros((p, SLOT - A_QK), F32)
    tc = jnp.concatenate([one, cos, cos, zp], 1)
    ts1 = jnp.concatenate([zn, -sin, zh, zp], 1)
    ts2 = jnp.concatenate([zn, zh, sin, zp], 1)
    return tc, ts1, ts2


def _mla_proj_kernel(*refs, with_kv):
    (x_ref, tc_ref, ts1_ref, ts2_ref, gn_ref, win_ref, gqa_ref, wuq_ref, gkv_ref, gq_ref) = refs[:10]
    if with_kv:
        wuk_ref, gk_ref, wuv_ref = refs[10:13]
        q_out, c_out, kpe_out, z_out, k_out, v_out = refs[13:]
    else:
        q_out, c_out, kpe_out, z_out = refs[10:]
    x = x_ref[0]
    h = _rms(x, gn_ref[...])
    proj = _dot(h.astype(BF16), win_ref[...])
    o1 = A_Q_LORA
    o2 = o1 + A_KV_LORA
    o3 = o2 + A_WIDTH
    qa = _rms(proj[:, :o1], gqa_ref[...])
    c = _rms(proj[:, o1:o2], gkv_ref[...])
    z_out[0] = proj[:, o2:o3]
    c_out[0] = c
    q = _dot(qa.astype(BF16), wuq_ref[...])
    tc = tc_ref[...]
    ts1 = ts1_ref[...]
    ts2 = ts2_ref[...]

    def rope(s):
        return s * tc + pltpu.roll(s, SLOT - A_ROPE // 2, 1) * ts1 + pltpu.roll(s, A_ROPE // 2, 1) * ts2

    def head_norm(s, g):
        ms = jnp.sum(s * s, -1, keepdims=True) * (1.0 / A_QK)
        return s * lax.rsqrt(ms + EPS) * g

    kslot = rope(proj[:, o3:o3 + SLOT])
    kpe_out[0] = kslot[:, A_NOPE:A_QK]
    gq = gq_ref[...]
    for hd in range(A_HEADS):
        sl = slice(hd * SLOT, (hd + 1) * SLOT)
        q_out[0, :, sl] = head_norm(rope(q[:, sl]), gq).astype(q_out.dtype)
    if with_kv:
        cb = c.astype(BF16)
        kn = _dot(cb, wuk_ref[...])
        v_out[0] = _dot(cb, wuv_ref[...]).astype(v_out.dtype)
        gk = gk_ref[...]
        for hd in range(A_HEADS):
            sl = slice(hd * SLOT, (hd + 1) * SLOT)
            k_out[0, :, sl] = head_norm(kn[:, sl] + kslot, gk).astype(k_out.dtype)


def _mla_project(x, tabs, w, *, with_kv, tm, q_dtype):
    b, s, _ = x.shape
    tm = min(tm, s)
    grid = (s // tm, b)
    full = lambda a: pl.BlockSpec(a.shape, lambda i, j: (0,) * a.ndim)
    row = lambda n: pl.BlockSpec((1, tm, n), lambda i, j: (j, i, 0))
    tab = pl.BlockSpec((tm, SLOT), lambda i, j: (i, 0))
    params = [w["a_norm"], w["w_in"], w["g_qa"], w["w_uq"], w["g_kv"], w["g_q"]]
    out_shape = [
        jax.ShapeDtypeStruct((b, s, A_QPAD), q_dtype),
        jax.ShapeDtypeStruct((b, s, A_KV_LORA), F32),
        jax.ShapeDtypeStruct((b, s, A_ROPE), F32),
        jax.ShapeDtypeStruct((b, s, A_WIDTH), F32),
    ]
    out_specs = [row(A_QPAD), row(A_KV_LORA), row(A_ROPE), row(A_WIDTH)]
    if with_kv:
        params += [w["w_uk"], w["g_k"], w["w_uv"]]
        out_shape += [jax.ShapeDtypeStruct((b, s, A_QPAD), BF16), jax.ShapeDtypeStruct((b, s, A_WIDTH), BF16)]
        out_specs += [row(A_QPAD), row(A_WIDTH)]
    return pl.pallas_call(
        functools.partial(_mla_proj_kernel, with_kv=with_kv),
        grid=grid,
        in_specs=[row(D_MODEL), tab, tab, tab] + [full(p) for p in params],
        out_specs=out_specs,
        out_shape=out_shape,
        compiler_params=_cparams(("parallel", "parallel")),
        name="mla_proj_kv" if with_kv else "mla_proj",
    )(x, *tabs, *params)


def _flash_kernel(qi_ref, ki_ref, fx_ref, q_ref, k_ref, v_ref, sh_ref, o_ref, m_ref, l_ref, acc_ref, *, tq):
    p = pl.program_id(1)
    qi = qi_ref[p]
    ki = ki_ref[p]
    fixed = fx_ref[0] == 1

    @pl.when(ki == 0)
    def _():
        m_ref[...] = jnp.full(m_ref.shape, -jnp.inf, F32)
        l_ref[...] = jnp.zeros(l_ref.shape, F32)
        acc_ref[...] = jnp.zeros(acc_ref.shape, F32)

    low = lax.broadcasted_iota(jnp.int32, (tq, SLOT), 1) < A_V

    def step(diag, fixed_shift):
        if diag:
            keep = lax.broadcasted_iota(jnp.int32, (tq, tq), 1) <= lax.broadcasted_iota(jnp.int32, (tq, tq), 0)
        if fixed_shift:
            shift = sh_ref[:, :1]
        for j in range(A_HEADS // 2):
            pv = []
            al = []
            vpair = v_ref[0, :, j * SLOT:(j + 1) * SLOT]
            for e in range(2):
                hd = 2 * j + e
                sl = slice(hd * SLOT, (hd + 1) * SLOT)
                s = _dot_nt(q_ref[0, :, sl], k_ref[0, :, sl])
                if diag:
                    s = jnp.where(keep, s, -jnp.inf)
                if fixed_shift:
                    pr = jnp.exp(s - shift)
                    l_ref[hd] = l_ref[hd] + jnp.sum(pr, -1, keepdims=True)
                else:
                    m_prev = m_ref[hd]
                    m_new = jnp.maximum(m_prev, jnp.max(s, -1, keepdims=True))
                    alpha = jnp.exp(m_prev - m_new)
                    pr = jnp.exp(s - m_new[:, :1])
                    l_ref[hd] = alpha * l_ref[hd] + jnp.sum(pr, -1, keepdims=True)
                    m_ref[hd] = m_new
                    al.append(alpha)
                pv.append(_dot(pr.astype(BF16), vpair))
            sl = slice(j * SLOT, (j + 1) * SLOT)
            if fixed_shift:
                acc_ref[:, sl] = acc_ref[:, sl] + jnp.where(low, pv[0], pv[1])
            else:
                acc_ref[:, sl] = jnp.where(low, al[0], al[1]) * acc_ref[:, sl] + jnp.where(low, pv[0], pv[1])

    def finish():
        for j in range(A_HEADS // 2):
            sl = slice(j * SLOT, (j + 1) * SLOT)
            linv = jnp.where(low, 1.0 / l_ref[2 * j], 1.0 / l_ref[2 * j + 1])
            o_ref[0, :, sl] = acc_ref[:, sl] * linv

    for fs in (True, False):
        mode = fixed if fs else jnp.logical_not(fixed)

        @pl.when(jnp.logical_and(mode, ki < qi))
        def _():
            step(False, fs)

        @pl.when(jnp.logical_and(mode, ki == qi))
        def _():
            step(True, fs)
            finish()


def _flash_attention(q, k, v, bound, *, tq):
    b, s, _ = q.shape
    tq = min(tq, s)
    nq = s // tq
    qi = jnp.asarray([i for i in range(nq) for _ in range(i + 1)], jnp.int32)
    ki = jnp.asarray([j for i in range(nq) for j in range(i + 1)], jnp.int32)
    fx = (bound <= FIXED_SHIFT_MAX).astype(jnp.int32).reshape(1)
    sh = jnp.full((1, SLOT), bound, F32)
    grid_spec = pltpu.PrefetchScalarGridSpec(
        num_scalar_prefetch=3,
        grid=(b, int(qi.shape[0])),
        in_specs=[
            pl.BlockSpec((1, tq, A_QPAD), lambda bi, p, qt, kt, fx: (bi, qt[p], 0)),
            pl.BlockSpec((1, tq, A_QPAD), lambda bi, p, qt, kt, fx: (bi, kt[p], 0)),
            pl.BlockSpec((1, tq, A_WIDTH), lambda bi, p, qt, kt, fx: (bi, kt[p], 0)),
            pl.BlockSpec((1, SLOT), lambda bi, p, qt, kt, fx: (0, 0)),
        ],
        out_specs=pl.BlockSpec((1, tq, A_WIDTH), lambda bi, p, qt, kt, fx: (bi, qt[p], 0)),
        scratch_shapes=[
            pltpu.VMEM((A_HEADS, tq, SLOT), F32),
            pltpu.VMEM((A_HEADS, tq, SLOT), F32),
            pltpu.VMEM((tq, A_WIDTH), F32),
        ],
    )
    return pl.pallas_call(
        functools.partial(_flash_kernel, tq=tq),
        grid_spec=grid_spec,
        out_shape=jax.ShapeDtypeStruct((b, s, A_WIDTH), F32),
        compiler_params=_cparams(("parallel", "arbitrary")),
        name="mla_flash",
    )(qi, ki, fx, q, k, v, sh)


def _paged_kernel(*refs, ppb, n_blocks):
    pt_ref = refs[0]
    q_ref, cn_ref, kn_ref, gk_ref, wukt_ref, wuv_ref = refs[1:7]
    lat_refs = refs[7:7 + ppb]
    kr_refs = refs[7 + ppb:7 + 2 * ppb]
    o_ref = refs[7 + 2 * ppb]
    wq_ref, qr_ref, m_ref, l_ref, acc_ref = refs[8 + 2 * ppb:]
    del pt_ref
    blk = pl.program_id(1)
    rows = A_HEADS * 8
    hn = A_HEADS * A_NOPE

    @pl.when(blk == 0)
    def _():
        m_ref[...] = jnp.full(m_ref.shape, -jnp.inf, F32)
        l_ref[...] = jnp.zeros(l_ref.shape, F32)
        acc_ref[...] = jnp.zeros(acc_ref.shape, F32)
        gk = gk_ref[...]
        for hd in range(A_HEADS):
            qs = q_ref[0, :, hd * SLOT:(hd + 1) * SLOT] * gk
            wq_ref[hd * A_NOPE:(hd + 1) * A_NOPE, :] = wukt_ref[hd]
            wq_ref[hn + hd * 8:hn + (hd + 1) * 8, :] = _dot(qs[:, :A_NOPE].astype(BF16), wukt_ref[hd]).astype(BF16)
            qr_ref[hd * 8:(hd + 1) * 8, :] = qs[:, A_NOPE:A_QK].astype(BF16)

    def scores(cb, krt):
        nk = cb.shape[0]
        res = _dot_nt(wq_ref[...], cb)
        ssq = []
        for hd in range(A_HEADS):
            kh = res[hd * A_NOPE:(hd + 1) * A_NOPE]
            ssq.append(jnp.broadcast_to(jnp.sum(kh * kh, 0, keepdims=True), (8, nk)))
        ssq = jnp.concatenate(ssq, 0) + jnp.sum(krt * krt, 0, keepdims=True)
        s = res[hn:] + _dot(qr_ref[...], krt.astype(BF16))
        return s * lax.rsqrt(ssq * (1.0 / A_QK) + EPS)

    def block(cbs, krts, new):
        sts = [scores(cb, krt) for cb, krt in zip(cbs, krts)]
        if new:
            nk = cbs[0].shape[0]
            tok = lax.broadcasted_iota(jnp.int32, (rows, nk), 0) & 7
            key = lax.broadcasted_iota(jnp.int32, (rows, nk), 1)
            sts = [jnp.where(key <= tok, st, -jnp.inf) for st in sts]
        ms = [jnp.max(st, -1, keepdims=True) for st in sts]
        prs = [jnp.exp(st - m) for st, m in zip(sts, ms)]
        ls = [jnp.sum(pr, -1, keepdims=True) for pr in prs]
        pvs = [_dot(pr.astype(BF16), cb) for pr, cb in zip(prs, cbs)]
        m_prev = m_ref[...]
        m_new = m_prev
        for m in ms:
            m_new = jnp.maximum(m_new, m)
        alpha = jnp.exp(m_prev - m_new)
        l_new = alpha * l_ref[...]
        acc = jnp.concatenate([alpha, alpha], 1) * acc_ref[...]
        for m, l, pv in zip(ms, ls, pvs):
            wgt = jnp.exp(m - m_new)
            l_new = l_new + wgt * l
            acc = acc + jnp.concatenate([wgt, wgt], 1) * pv
        m_ref[...] = m_new
        l_ref[...] = l_new
        acc_ref[...] = acc

    sub = max(ppb // PAGED_PARTS, 1)
    block([jnp.concatenate([r[0, 0].astype(BF16) for r in lat_refs[i:i + sub]], 0) for i in range(0, ppb, sub)],
          [jnp.concatenate([r[0, 0] for r in kr_refs[i:i + sub]], 1) for i in range(0, ppb, sub)], False)

    @pl.when(blk == n_blocks - 1)
    def _():
        block([cn_ref[0].astype(BF16)], [kn_ref[0]], True)
        o_lat = (acc_ref[...] * (1.0 / l_ref[...][:, :1])).astype(BF16)
        for hd in range(A_HEADS):
            o_ref[0, :, hd * A_V:(hd + 1) * A_V] = _dot(o_lat[hd * 8:(hd + 1) * 8], wuv_ref[hd])


def _paged_attention(q, c_new, kpe_new_t, cache_latent, cache_krope_t, la, page_table, w, *, ppb):
    n = q.shape[0]
    n_pages = page_table.shape[1]
    ppb = min(ppb, n_pages)
    n_blocks = n_pages // ppb
    full = lambda a: pl.BlockSpec(a.shape, lambda i, j, pt: (0,) * a.ndim)
    rows = A_HEADS * 8
    params = [w["g_k"], w["w_ukt"], w["w_uv_h"]]

    def lat_spec(i):
        return pl.BlockSpec((1, 1, PAGE, A_KV_LORA), lambda s, j, pt, i=i: (la, pt[s, j * ppb + i], 0, 0))

    def kr_spec(i):
        return pl.BlockSpec((1, 1, A_ROPE, PAGE), lambda s, j, pt, i=i: (la, pt[s, j * ppb + i], 0, 0))

    grid_spec = pltpu.PrefetchScalarGridSpec(
        num_scalar_prefetch=1,
        grid=(n, n_blocks),
        in_specs=[
            pl.BlockSpec((1, 8, A_QPAD), lambda s, j, pt: (s, 0, 0)),
            pl.BlockSpec((1, PAGE, A_KV_LORA), lambda s, j, pt: (s, 0, 0)),
            pl.BlockSpec((1, A_ROPE, PAGE), lambda s, j, pt: (s, 0, 0)),
        ] + [full(p) for p in params] + [lat_spec(i) for i in range(ppb)] + [kr_spec(i) for i in range(ppb)],
        out_specs=pl.BlockSpec((1, 8, A_WIDTH), lambda s, j, pt: (s, 0, 0)),
        scratch_shapes=[
            pltpu.VMEM((A_HEADS * A_NOPE + rows, A_KV_LORA), BF16),
            pltpu.VMEM((rows, A_ROPE), BF16),
            pltpu.VMEM((rows, SLOT), F32),
            pltpu.VMEM((rows, SLOT), F32),
            pltpu.VMEM((rows, A_KV_LORA), F32),
        ],
    )
    return pl.pallas_call(
        functools.partial(_paged_kernel, ppb=ppb, n_blocks=n_blocks),
        grid_spec=grid_spec,
        out_shape=jax.ShapeDtypeStruct((n, 8, A_WIDTH), F32),
        compiler_params=_cparams(("parallel", "arbitrary")),
        name="mla_paged",
    )(page_table, q, c_new, kpe_new_t, *params, *([cache_latent] * ppb), *([cache_krope_t] * ppb))


def _gated_out_kernel(x_ref, o_ref, z_ref, w_ref, y_ref):
    gated = o_ref[...] * _silu(z_ref[...])
    y_ref[...] = x_ref[...] + _dot(gated.astype(BF16), w_ref[...])


def _gated_out(x, o, z, w_o, *, tm):
    r = x.shape[0]
    tm = min(tm, r)
    row = lambda n: pl.BlockSpec((tm, n), lambda i: (i, 0))
    return pl.pallas_call(
        _gated_out_kernel,
        grid=(r // tm,),
        in_specs=[row(D_MODEL), row(o.shape[1]), row(z.shape[1]), pl.BlockSpec(w_o.shape, lambda i: (0, 0))],
        out_specs=row(D_MODEL),
        out_shape=jax.ShapeDtypeStruct((r, D_MODEL), F32),
        compiler_params=_cparams(("parallel",)),
        name="gated_out",
    )(x, o, z, w_o)


def _gdn_proj_kernel(x_ref, gn_ref, w_ref, wabt_ref, qkv_out, z_out, ab_out, abt_out):
    hb = _rms(x_ref[...], gn_ref[...]).astype(BF16)
    proj = _dot(hb, w_ref[...])
    o1 = B_CONV_CH
    o2 = o1 + B_WIDTH
    qkv_out[...] = proj[:, :o1]
    z_out[...] = proj[:, o1:o2]
    ab_out[...] = proj[:, o2:o2 + 2 * B_HEADS]
    abt_out[...] = _dot_nt(wabt_ref[...], hb)


def _gdn_project(x, w, *, tm):
    r = x.shape[0]
    tm = min(tm, r)
    row = lambda n: pl.BlockSpec((tm, n), lambda i: (i, 0))
    full = lambda a: pl.BlockSpec(a.shape, lambda i: (0,) * a.ndim)
    params = [w["b_norm"], w["w_in"], w["w_abt"]]
    return pl.pallas_call(
        _gdn_proj_kernel,
        grid=(r // tm,),
        in_specs=[row(D_MODEL)] + [full(p) for p in params],
        out_specs=[row(B_CONV_CH), row(B_WIDTH), row(2 * B_HEADS), pl.BlockSpec((2 * B_HEADS, tm), lambda i: (0, i))],
        out_shape=[
            jax.ShapeDtypeStruct((r, B_CONV_CH), F32),
            jax.ShapeDtypeStruct((r, B_WIDTH), F32),
            jax.ShapeDtypeStruct((r, 2 * B_HEADS), F32),
            jax.ShapeDtypeStruct((2 * B_HEADS, r), F32),
        ],
        compiler_params=_cparams(("parallel",)),
        name="gdn_proj",
    )(x, *params)


def _gdn_chunk_kernel(qkv_ref, ab_ref, abt_ref, conv0_ref, st0_ref, wc_ref, alr_ref, dtr_ref, alc_ref, dtc_ref,
                      go_ref, exp_ref, tri_ref, triu_ref, off_ref, avg_ref, o_ref, st_out, ext_ref, feat_ref, gcx_ref,
                      bx_ref, st_ref, lm_ref, inv_ref, t1_ref, in_ref, *, t, valid):
    si = pl.program_id(1)
    hk = B_HEADS * B_DK
    heads = range(B_HEADS)
    qsl = lambda hd: slice(hd * B_DK, (hd + 1) * B_DK)
    ksl = lambda hd: slice(hk + hd * B_DK, hk + (hd + 1) * B_DK)
    vsl = lambda hd: slice(2 * hk + hd * B_DV, 2 * hk + (hd + 1) * B_DV)

    @pl.when(si == 0)
    def _():
        ext_ref[0:HALO] = conv0_ref[0]
        st_ref[...] = st0_ref[0]

    ext_ref[HALO:HALO + t] = qkv_ref[0]
    conv = ext_ref[HALO - 3:HALO - 3 + t] * wc_ref[0:1]
    for j in range(1, B_CONV):
        conv = conv + ext_ref[HALO - 3 + j:HALO - 3 + j + t] * wc_ref[j:j + 1]
    feat_ref[...] = _silu(conv)
    ext_ref[0:HALO] = ext_ref[t:t + HALO]

    ab = ab_ref[0]
    g = -jnp.exp(alr_ref[...]) * _softplus(ab[:, :B_HEADS] + dtr_ref[...])
    beta = jax.nn.sigmoid(ab[:, B_HEADS:])
    if valid < t:
        live = lax.broadcasted_iota(jnp.int32, (t, B_HEADS), 0) < valid
        g = jnp.where(live, g, 0.0)
        beta = jnp.where(live, beta, 0.0)
    bx_ref[...] = _dot_sel(exp_ref[...], beta, sel_first=False)
    gcx_ref[...] = _dot_sel(tri_ref[...], _dot_sel(exp_ref[...], g, sel_first=False), sel_first=True)
    gt = -jnp.exp(alc_ref[...]) * _softplus(abt_ref[0, 0][:B_HEADS] + dtc_ref[...])
    if valid < t:
        gt = jnp.where(lax.broadcasted_iota(jnp.int32, (B_HEADS, t), 1) < valid, gt, 0.0)
    gct = _dot_sel(triu_ref[...], gt, sel_first=False)

    ii = lax.broadcasted_iota(jnp.int32, (t, t), 0)
    jj = lax.broadcasted_iota(jnp.int32, (t, t), 1)
    incl = ii >= jj
    strict = ii > jj
    eye = (ii == jj).astype(F32)
    pair = (ii >> 1) == (jj >> 1)

    def cols(x):
        return x[:, :t] if t <= B_DK else jnp.tile(x, (1, t // B_DK))

    for hd in heads:
        qh = feat_ref[:, qsl(hd)]
        kh = feat_ref[:, ksl(hd)]
        qh = qh * lax.rsqrt(jnp.sum(qh * qh, -1, keepdims=True) + EPS) * (B_DK ** -0.5)
        kh = kh * lax.rsqrt(jnp.sum(kh * kh, -1, keepdims=True) + EPS)
        feat_ref[:, qsl(hd)] = qh
        feat_ref[:, ksl(hd)] = kh
        decay = jnp.exp(jnp.where(incl, cols(gcx_ref[:, qsl(hd)]) - gct[hd:hd + 1, :], -jnp.inf))
        khb = kh.astype(BF16)
        lmat = jnp.where(strict, _dot_nt((kh * bx_ref[:, qsl(hd)]).astype(BF16), khb) * decay, 0.0)
        lm_ref[hd] = lmat.astype(BF16)
        in_ref[hd] = (_dot_nt(qh.astype(BF16), khb) * decay).astype(BF16)
        inv_ref[hd] = (eye - jnp.where(pair, lmat, 0.0)).astype(BF16)

    for lv in range(off_ref.shape[0]):
        for hd in heads:
            t1_ref[hd] = _dot(lm_ref[hd] * off_ref[lv], inv_ref[hd]).astype(BF16)
        for hd in heads:
            inv_ref[hd] = inv_ref[hd] - _dot(inv_ref[hd], t1_ref[hd]).astype(BF16)

    go = go_ref[...]
    us, ws, egs = [], [], []
    for hd in heads:
        invb = inv_ref[hd]
        bc = bx_ref[:, qsl(hd)]
        eg = jnp.exp(gcx_ref[:, qsl(hd)])
        kb = feat_ref[:, ksl(hd)] * bc
        us.append(_dot(invb, (feat_ref[:, vsl(hd)] * bc).astype(BF16)))
        ws.append(_dot(invb, (kb * eg).astype(BF16)).astype(BF16))
        egs.append(eg)
    vns = []
    for hd in heads:
        vns.append((us[hd] - _dot(ws[hd], st_ref[hd].astype(BF16))).astype(BF16))
    for hd in heads:
        st = st_ref[hd]
        o = _dot((feat_ref[:, qsl(hd)] * egs[hd]).astype(BF16), st.astype(BF16)) + _dot(in_ref[hd], vns[hd])
        gcc = gcx_ref[:, qsl(hd)]
        glast = gcc[t - 1:t, :]
        kdec = feat_ref[:, ksl(hd)] * jnp.exp(glast - gcc)
        st_ref[hd] = st * jnp.exp(glast) + _dot_tn(kdec.astype(BF16), vns[hd])
        o2 = o * o
        o2h = o2.astype(BF16)
        ms = _dot(o2h, avg_ref[...]) + _dot((o2 - o2h.astype(F32)).astype(BF16), avg_ref[...])
        o_ref[0, :, qsl(hd)] = o * lax.rsqrt(ms + EPS) * go

    @pl.when(si == pl.num_programs(1) - 1)
    def _():
        st_out[0] = st_ref[...]


def _gdn_chunked(qkv, ab, abt, conv0, st0, w, *, t, valid):
    b, s, _ = qkv.shape
    t = min(t, s)
    full = lambda a: pl.BlockSpec(a.shape, lambda i, j: (0,) * a.ndim)
    expand = jnp.repeat(jnp.eye(B_HEADS, dtype=BF16), B_DV, axis=1)
    pos = jnp.arange(t)
    tri = (pos[:, None] >= pos[None, :]).astype(BF16)
    triu = (pos[:, None] <= pos[None, :]).astype(BF16)
    blk = lambda lg: pos >> lg
    off = jnp.stack([((blk(lg)[:, None] - blk(lg)[None, :] == 1) & ((blk(lg)[:, None] & 1) == 1)).astype(BF16)
                     for lg in range(1, t.bit_length() - 1)])
    avg = jnp.full((B_DV, B_DV), 1.0 / B_DV, BF16)
    params = [w["w_conv"], w["a_log_r"], w["dt_r"], w["a_log_c"], w["dt_c"], w["g_o"], expand, tri, triu, off, avg]
    return pl.pallas_call(
        functools.partial(_gdn_chunk_kernel, t=t, valid=valid),
        grid=(b, s // t),
        in_specs=[
            pl.BlockSpec((1, t, B_CONV_CH), lambda i, j: (i, j, 0)),
            pl.BlockSpec((1, t, 2 * B_HEADS), lambda i, j: (i, j, 0)),
            pl.BlockSpec((1, 1, 2 * B_HEADS, t), lambda i, j: (i, j, 0, 0)),
            pl.BlockSpec((1, HALO, B_CONV_CH), lambda i, j: (i, 0, 0)),
            pl.BlockSpec((1, B_HEADS, B_DK, B_DV), lambda i, j: (i, 0, 0, 0)),
        ] + [full(p) for p in params],
        out_specs=[
            pl.BlockSpec((1, t, B_WIDTH), lambda i, j: (i, j, 0)),
            pl.BlockSpec((1, B_HEADS, B_DK, B_DV), lambda i, j: (i, 0, 0, 0)),
        ],
        out_shape=[
            jax.ShapeDtypeStruct((b, s, B_WIDTH), F32),
            jax.ShapeDtypeStruct((b, B_HEADS, B_DK, B_DV), F32),
        ],
        scratch_shapes=[
            pltpu.VMEM((t + HALO, B_CONV_CH), F32),
            pltpu.VMEM((t, B_CONV_CH), F32),
            pltpu.VMEM((t, B_WIDTH), F32),
            pltpu.VMEM((t, B_WIDTH), F32),
            pltpu.VMEM((B_HEADS, B_DK, B_DV), F32),
            pltpu.VMEM((B_HEADS, t, t), BF16),
            pltpu.VMEM((B_HEADS, t, t), BF16),
            pltpu.VMEM((B_HEADS, t, t), BF16),
            pltpu.VMEM((B_HEADS, t, t), BF16),
        ],
        compiler_params=_cparams(("parallel", "arbitrary")),
        name="gdn_chunk",
    )(qkv, ab, abt, conv0, st0, *params)


def _pad_heads(wm, n_in):
    d = wm.shape[-1]
    return jnp.pad(wm, ((0, 0), (0, 0), (0, SLOT - d))).reshape(n_in, A_HEADS * SLOT)


def _mla_weights(a_norm, a_w_in, a_g_qa, a_w_uq, a_g_kv, a_w_uk, a_w_uv, a_g_q, a_g_k, a_w_o):
    o1 = A_Q_LORA
    o2 = o1 + A_KV_LORA
    o3 = o2 + A_ROPE
    kslot = jnp.pad(a_w_in[:, o2:o3], ((0, 0), (A_NOPE, SLOT - A_QK)))
    w_in = jnp.concatenate([a_w_in[:, :o2], a_w_in[:, o3:], kslot], 1).astype(BF16)
    pad_gain = lambda g: jnp.pad(g, (0, SLOT - A_QK)).reshape(1, SLOT)
    return {
        "a_norm": a_norm.reshape(1, D_MODEL),
        "w_in": w_in,
        "g_qa": a_g_qa.reshape(1, A_Q_LORA),
        "w_uq": _pad_heads(a_w_uq.reshape(A_Q_LORA, A_HEADS, A_QK), A_Q_LORA).astype(BF16),
        "g_kv": a_g_kv.reshape(1, A_KV_LORA),
        "g_q": pad_gain(a_g_q * (A_QK ** -0.5)),
        "g_k": pad_gain(a_g_k),
        "w_uk": _pad_heads(a_w_uk, A_KV_LORA).astype(BF16),
        "w_ukt": jnp.transpose(a_w_uk, (1, 2, 0)).astype(BF16),
        "w_uv": a_w_uv.reshape(A_KV_LORA, A_WIDTH).astype(BF16),
        "w_uv_h": jnp.transpose(a_w_uv, (1, 0, 2)).astype(BF16),
        "w_o": a_w_o.astype(BF16),
    }


def _gdn_weights(b_norm, b_w_in, b_w_conv, b_a_log, b_dt_bias, b_g_o, b_w_o):
    o2 = B_CONV_CH + B_WIDTH
    w_in = jnp.pad(b_w_in, ((0, 0), (0, SLOT - 2 * B_HEADS))).astype(BF16)
    return {
        "b_norm": b_norm.reshape(1, D_MODEL),
        "w_in": w_in,
        "w_abt": b_w_in[:, o2:].T.astype(BF16),
        "w_conv": b_w_conv,
        "a_log_r": b_a_log.reshape(1, B_HEADS),
        "dt_r": b_dt_bias.reshape(1, B_HEADS),
        "a_log_c": b_a_log.reshape(B_HEADS, 1),
        "dt_c": b_dt_bias.reshape(B_HEADS, 1),
        "g_o": b_g_o.reshape(1, B_DV),
        "w_o": b_w_o.astype(BF16),
    }


def _mla_layer(xp, xs, cache_latent, cache_krope, la, page_table, w, tabs_p, tabs_s):
    b, s, _ = xp.shape
    n, t, _ = xs.shape
    q, c_p, kpe_p, z, k, v = _mla_project(xp, tabs_p, w, with_kv=True, tm=256, q_dtype=BF16)
    bound = A_QK * BF16_NORM_SLACK * jnp.max(jnp.abs(w["g_q"])) * jnp.max(jnp.abs(w["g_k"]))
    o = _flash_attention(q, k, v, bound, tq=512)
    yp = _gated_out(xp.reshape(b * s, D_MODEL), o.reshape(b * s, A_WIDTH), z.reshape(b * s, A_WIDTH), w["w_o"], tm=512)

    q, c_s, kpe_s, z = _mla_project(xs.reshape(1, n * t, D_MODEL), tabs_s, w, with_kv=False, tm=256, q_dtype=F32)
    c_s = c_s.reshape(n, t, A_KV_LORA)
    kpe_s = kpe_s.reshape(n, t, A_ROPE)
    q8 = jnp.pad(q.reshape(n, t, A_QPAD), ((0, 0), (0, 8 - t), (0, 0)))
    c_new = jnp.pad(c_s, ((0, 0), (0, PAGE - t), (0, 0)))
    kpe_new_t = jnp.pad(jnp.swapaxes(kpe_s, 1, 2), ((0, 0), (0, 0), (0, PAGE - t)))
    o = _paged_attention(q8, c_new, kpe_new_t, cache_latent, jnp.swapaxes(cache_krope, 2, 3), la, page_table, w,
                         ppb=16)
    ys = _gated_out(xs.reshape(n * t, D_MODEL), o[:, :t].reshape(n * t, A_WIDTH), z.reshape(n * t, A_WIDTH),
                    w["w_o"], tm=512)
    return yp.reshape(b, s, D_MODEL), ys.reshape(n, t, D_MODEL), c_p, kpe_p, c_s, kpe_s


def _gdn_layer(xp, xs, state_conv, state_ssm, w):
    b, s, _ = xp.shape
    n, t, _ = xs.shape
    tc = min(256, s)
    qkv, z, ab, abt = _gdn_project(xp.reshape(b * s, D_MODEL), w, tm=256)
    qkv = qkv.reshape(b, s, B_CONV_CH)
    abt = abt.reshape(2 * B_HEADS, b, s // tc, tc).transpose(1, 2, 0, 3)
    o, st_p = _gdn_chunked(qkv, ab.reshape(b, s, 2 * B_HEADS), abt,
                           jnp.zeros((b, HALO, B_CONV_CH), F32), jnp.zeros((b, B_HEADS, B_DK, B_DV), F32),
                           w, t=tc, valid=tc)
    yp = _gated_out(xp.reshape(b * s, D_MODEL), o.reshape(b * s, B_WIDTH), z, w["w_o"], tm=512)
    conv_p = qkv[:, s - (B_CONV - 1):]

    qkv, z, ab, abt = _gdn_project(xs.reshape(n * t, D_MODEL), w, tm=256)
    qkv = qkv.reshape(n, t, B_CONV_CH)
    pad_t = lambda a: jnp.pad(a, ((0, 0), (0, 8 - t), (0, 0)))
    abt = jnp.pad(abt.reshape(2 * B_HEADS, n, t).transpose(1, 0, 2), ((0, 0), (0, 0), (0, 8 - t)))
    conv0 = jnp.pad(state_conv, ((0, 0), (HALO - (B_CONV - 1), 0), (0, 0)))
    o, st_s = _gdn_chunked(pad_t(qkv), pad_t(ab.reshape(n, t, 2 * B_HEADS)), abt.reshape(n, 1, 2 * B_HEADS, 8),
                           conv0, state_ssm, w, t=8, valid=t)
    ys = _gated_out(xs.reshape(n * t, D_MODEL), o[:, :t].reshape(n * t, B_WIDTH), z, w["w_o"], tm=512)
    conv_s = jnp.concatenate([state_conv, qkv], 1)[:, -(B_CONV - 1):]
    return yp.reshape(b, s, D_MODEL), ys.reshape(n, t, D_MODEL), conv_p, st_p, conv_s, st_s


def kernel(x_prompt, x_sample, cache_latent, cache_krope, page_table, state_conv, state_ssm,
           a_norm, a_w_in, a_g_qa, a_w_uq, a_g_kv, a_w_uk, a_w_uv, a_g_q, a_g_k, a_w_o,
           b_norm, b_w_in, b_w_conv, b_a_log, b_dt_bias, b_g_o, b_w_o):
    s = x_prompt.shape[1]
    n, t, _ = x_sample.shape
    past = page_table.shape[1] * PAGE
    p_pad = -(-(s + t) // SLOT) * SLOT
    pos = jnp.concatenate([jnp.arange(s), past + jnp.arange(t), jnp.zeros((p_pad - s - t,), jnp.int32)]).astype(F32)
    tabs = _rope_tables(pos)
    tabs_p = tuple(tb[:s] for tb in tabs)
    tabs_s = tuple(jnp.tile(tb[s:s + t], (n, 1)) for tb in tabs)

    wa = _mla_weights(a_norm[0], a_w_in[0], a_g_qa[0], a_w_uq[0], a_g_kv[0], a_w_uk[0], a_w_uv[0], a_g_q[0],
                      a_g_k[0], a_w_o[0])
    wb = _gdn_weights(b_norm[0], b_w_in[0], b_w_conv[0], b_a_log[0], b_dt_bias[0], b_g_o[0], b_w_o[0])

    xp, xs, lat_p, kpe_p, lat_s, kpe_s = _mla_layer(x_prompt, x_sample, cache_latent, cache_krope, 0, page_table,
                                                    wa, tabs_p, tabs_s)
    xp, xs, conv_p, ssm_p, conv_s, ssm_s = _gdn_layer(xp, xs, state_conv[0], state_ssm[0], wb)
    return (xp, xs, lat_p[None], kpe_p[None], lat_s[None], kpe_s[None],
            conv_p[None], ssm_p[None], conv_s[None], ssm_s[None])
```

```python
import functools

import jax
import jax.numpy as jnp
from jax import lax
from jax.experimental import pallas as pl
from jax.experimental.pallas import tpu as pltpu

F32 = jnp.float32
BF16 = jnp.bfloat16
EPS = 1e-6

D_MODEL = 1024
PAGE = 128
A_HEADS = 8
A_NOPE = 64
A_ROPE = 32
A_QK = A_NOPE + A_ROPE
A_V = 64
A_Q_LORA = 384
A_KV_LORA = 256
A_WIDTH = A_HEADS * A_V
ROPE_THETA = 10000.0
SLOT = 128
A_QPAD = A_HEADS * SLOT
B_HEADS = 8
B_DK = 64
B_DV = 64
B_WIDTH = B_HEADS * B_DV
B_CONV = 4
B_CONV_CH = 2 * B_HEADS * B_DK + B_WIDTH
HALO = 8

VMEM_LIMIT = 56 * 1024 * 1024
FIXED_SHIFT_MAX = 30.0
BF16_NORM_SLACK = 1.02
PAGED_PARTS = 2


def _cparams(sem):
    return pltpu.CompilerParams(dimension_semantics=sem, vmem_limit_bytes=VMEM_LIMIT)


def _dot(a, b):
    return jnp.dot(a, b, preferred_element_type=F32)


def _dot_nt(a, b):
    return lax.dot_general(a, b, (((1,), (1,)), ((), ())), preferred_element_type=F32)


def _dot_tn(a, b):
    return lax.dot_general(a, b, (((0,), (0,)), ((), ())), preferred_element_type=F32)


def _dot_sel(sel_bf16, x, *, sel_first):
    x1 = x.astype(BF16)
    r1 = x - x1.astype(F32)
    x2 = r1.astype(BF16)
    x3 = (r1 - x2.astype(F32)).astype(BF16)
    if sel_first:
        return _dot(sel_bf16, x1) + _dot(sel_bf16, x2) + _dot(sel_bf16, x3)
    return _dot(x1, sel_bf16) + _dot(x2, sel_bf16) + _dot(x3, sel_bf16)


def _rms(x, g):
    return x * lax.rsqrt(jnp.mean(x * x, -1, keepdims=True) + EPS) * g


def _silu(x):
    return x * jax.nn.sigmoid(x)


def _softplus(x):
    return jnp.maximum(x, 0.0) + jnp.log(1.0 + jnp.exp(-jnp.abs(x)))


def _rope_table_kernel(pos_ref, inv_ref, cos_ref, sin_ref):
    ang = inv_ref[...] * pos_ref[...]
    cos_ref[...] = jnp.cos(ang)
    sin_ref[...] = jnp.sin(ang)


def _rope_tables(pos):
    half = A_ROPE // 2
    p = pos.shape[0]
    inv = (ROPE_THETA ** (-jnp.arange(half, dtype=F32) / half)).reshape(half, 1)
    cos_t, sin_t = pl.pallas_call(
        _rope_table_kernel,
        out_shape=(jax.ShapeDtypeStruct((half, p), F32),) * 2,
        name="rope_tables",
    )(pos.reshape(1, p), inv)
    cos = cos_t.T
    sin = sin_t.T
    one = jnp.ones((p, A_NOPE), F32)
    zn = jnp.zeros((p, A_NOPE), F32)
    zh = jnp.zeros((p, half), F32)
    zp = jnp.zeros((p, SLOT - A_QK), F32)
    tc = jnp.concatenate([one, cos, cos, zp], 1)
    ts1 = jnp.concatenate([zn, -sin, zh, zp], 1)
    ts2 = jnp.concatenate([zn, zh, sin, zp], 1)
    return tc, ts1, ts2


def _mla_proj_kernel(*refs, with_kv):
    (x_ref, tc_ref, ts1_ref, ts2_ref, gn_ref, win_ref, gqa_ref, wuq_ref, gkv_ref, gq_ref) = refs[:10]
    if with_kv:
        wuk_ref, gk_ref, wuv_ref = refs[10:13]
        q_out, c_out, kpe_out, z_out, k_out, v_out = refs[13:]
    else:
        q_out, c_out, kpe_out, z_out = refs[10:]
    x = x_ref[0]
    h = _rms(x, gn_ref[...])
    proj = _dot(h.astype(BF16), win_ref[...])
    o1 = A_Q_LORA
    o2 = o1 + A_KV_LORA
    o3 = o2 + A_WIDTH
    qa = _rms(proj[:, :o1], gqa_ref[...])
    c = _rms(proj[:, o1:o2], gkv_ref[...])
    z_out[0] = proj[:, o2:o3]
    c_out[0] = c
    q = _dot(qa.astype(BF16), wuq_ref[...])
    tc = tc_ref[...]
    ts1 = ts1_ref[...]
    ts2 = ts2_ref[...]

    def rope(s):
        return s * tc + pltpu.roll(s, SLOT - A_ROPE // 2, 1) * ts1 + pltpu.roll(s, A_ROPE // 2, 1) * ts2

    def head_norm(s, g):
        ms = jnp.sum(s * s, -1, keepdims=True) * (1.0 / A_QK)
        return s * lax.rsqrt(ms + EPS) * g

    kslot = rope(proj[:, o3:o3 + SLOT])
    kpe_out[0] = kslot[:, A_NOPE:A_QK]
    gq = gq_ref[...]
    for hd in range(A_HEADS):
        sl = slice(hd * SLOT, (hd + 1) * SLOT)
        q_out[0, :, sl] = head_norm(rope(q[:, sl]), gq).astype(q_out.dtype)
    if with_kv:
        cb = c.astype(BF16)
        kn = _dot(cb, wuk_ref[...])
        v_out[0] = _dot(cb, wuv_ref[...]).astype(v_out.dtype)
        gk = gk_ref[...]
        for hd in range(A_HEADS):
            sl = slice(hd * SLOT, (hd + 1) * SLOT)
            k_out[0, :, sl] = head_norm(kn[:, sl] + kslot, gk).astype(k_out.dtype)


def _mla_project(x, tabs, w, *, with_kv, tm, q_dtype):
    b, s, _ = x.shape
    tm = min(tm, s)
    grid = (s // tm, b)
    full = lambda a: pl.BlockSpec(a.shape, lambda i, j: (0,) * a.ndim)
    row = lambda n: pl.BlockSpec((1, tm, n), lambda i, j: (j, i, 0))
    tab = pl.BlockSpec((tm, SLOT), lambda i, j: (i, 0))
    params = [w["a_norm"], w["w_in"], w["g_qa"], w["w_uq"], w["g_kv"], w["g_q"]]
    out_shape = [
        jax.ShapeDtypeStruct((b, s, A_QPAD), q_dtype),
        jax.ShapeDtypeStruct((b, s, A_KV_LORA), F32),
        jax.ShapeDtypeStruct((b, s, A_ROPE), F32),
        jax.ShapeDtypeStruct((b, s, A_WIDTH), F32),
    ]
    out_specs = [row(A_QPAD), row(A_KV_LORA), row(A_ROPE), row(A_WIDTH)]
    if with_kv:
        params += [w["w_uk"], w["g_k"], w["w_uv"]]
        out_shape += [jax.ShapeDtypeStruct((b, s, A_QPAD), BF16), jax.ShapeDtypeStruct((b, s, A_WIDTH), BF16)]
        out_specs += [row(A_QPAD), row(A_WIDTH)]
    return pl.pallas_call(
        functools.partial(_mla_proj_kernel, with_kv=with_kv),
        grid=grid,
        in_specs=[row(D_MODEL), tab, tab, tab] + [full(p) for p in params],
        out_specs=out_specs,
        out_shape=out_shape,
        compiler_params=_cparams(("parallel", "parallel")),
        name="mla_proj_kv" if with_kv else "mla_proj",
    )(x, *tabs, *params)


def _flash_kernel(qi_ref, ki_ref, fx_ref, q_ref, k_ref, v_ref, sh_ref, o_ref, m_ref, l_ref, acc_ref, *, tq):
    p = pl.program_id(1)
    qi = qi_ref[p]
    ki = ki_ref[p]
    fixed = fx_ref[0] == 1

    @pl.when(ki == 0)
    def _():
        m_ref[...] = jnp.full(m_ref.shape, -jnp.inf, F32)
        l_ref[...] = jnp.zeros(l_ref.shape, F32)
        acc_ref[...] = jnp.zeros(acc_ref.shape, F32)

    low = lax.broadcasted_iota(jnp.int32, (tq, SLOT), 1) < A_V

    def step(diag, fixed_shift):
        if diag:
            keep = lax.broadcasted_iota(jnp.int32, (tq, tq), 1) <= lax.broadcasted_iota(jnp.int32, (tq, tq), 0)
        if fixed_shift:
            shift = sh_ref[:, :1]
        for j in range(A_HEADS // 2):
            pv = []
            al = []
            vpair = v_ref[0, :, j * SLOT:(j + 1) * SLOT]
            for e in range(2):
                hd = 2 * j + e
                sl = slice(hd * SLOT, (hd + 1) * SLOT)
                s = _dot_nt(q_ref[0, :, sl], k_ref[0, :, sl])
                if diag:
                    s = jnp.where(keep, s, -jnp.inf)
                if fixed_shift:
                    pr = jnp.exp(s - shift)
                    l_ref[hd] = l_ref[hd] + jnp.sum(pr, -1, keepdims=True)
                else:
                    m_prev = m_ref[hd]
                    m_new = jnp.maximum(m_prev, jnp.max(s, -1, keepdims=True))
                    alpha = jnp.exp(m_prev - m_new)
                    pr = jnp.exp(s - m_new[:, :1])
                    l_ref[hd] = alpha * l_ref[hd] + jnp.sum(pr, -1, keepdims=True)
                    m_ref[hd] = m_new
                    al.append(alpha)
                pv.append(_dot(pr.astype(BF16), vpair))
            sl = slice(j * SLOT, (j + 1) * SLOT)
            if fixed_shift:
                acc_ref[:, sl] = acc_ref[:, sl] + jnp.where(low, pv[0], pv[1])
            else:
                acc_ref[:, sl] = jnp.where(low, al[0], al[1]) * acc_ref[:, sl] + jnp.where(low, pv[0], pv[1])

    def finish():
        for j in range(A_HEADS // 2):
            sl = slice(j * SLOT, (j + 1) * SLOT)
            linv = jnp.where(low, 1.0 / l_ref[2 * j], 1.0 / l_ref[2 * j + 1])
            o_ref[0, :, sl] = acc_ref[:, sl] * linv

    for fs in (True, False):
        mode = fixed if fs else jnp.logical_not(fixed)

        @pl.when(jnp.logical_and(mode, ki < qi))
        def _():
            step(False, fs)

        @pl.when(jnp.logical_and(mode, ki == qi))
        def _():
            step(True, fs)
            finish()


def _flash_attention(q, k, v, bound, *, tq):
    b, s, _ = q.shape
    tq = min(tq, s)
    nq = s // tq
    qi = jnp.asarray([i for i in range(nq) for _ in range(i + 1)], jnp.int32)
    ki = jnp.asarray([j for i in range(nq) for j in range(i + 1)], jnp.int32)
    fx = (bound <= FIXED_SHIFT_MAX).astype(jnp.int32).reshape(1)
    sh = jnp.full((1, SLOT), bound, F32)
    grid_spec = pltpu.PrefetchScalarGridSpec(
        num_scalar_prefetch=3,
        grid=(b, int(qi.shape[0])),
        in_specs=[
            pl.BlockSpec((1, tq, A_QPAD), lambda bi, p, qt, kt, fx: (bi, qt[p], 0)),
            pl.BlockSpec((1, tq, A_QPAD), lambda bi, p, qt, kt, fx: (bi, kt[p], 0)),
            pl.BlockSpec((1, tq, A_WIDTH), lambda bi, p, qt, kt, fx: (bi, kt[p], 0)),
            pl.BlockSpec((1, SLOT), lambda bi, p, qt, kt, fx: (0, 0)),
        ],
        out_specs=pl.BlockSpec((1, tq, A_WIDTH), lambda bi, p, qt, kt, fx: (bi, qt[p], 0)),
        scratch_shapes=[
            pltpu.VMEM((A_HEADS, tq, SLOT), F32),
            pltpu.VMEM((A_HEADS, tq, SLOT), F32),
            pltpu.VMEM((tq, A_WIDTH), F32),
        ],
    )
    return pl.pallas_call(
        functools.partial(_flash_kernel, tq=tq),
        grid_spec=grid_spec,
        out_shape=jax.ShapeDtypeStruct((b, s, A_WIDTH), F32),
        compiler_params=_cparams(("parallel", "arbitrary")),
        name="mla_flash",
    )(qi, ki, fx, q, k, v, sh)


def _paged_kernel(pt_ref, q_ref, cn_ref, kn_ref, gk_ref, wukt_ref, wuv_ref, lat_hbm, kr_hbm, o_ref,
                  wq_ref, qr_ref, m_ref, l_ref, acc_ref, lat_buf, kr_buf, sem, *, la, ppb, n_blocks):
    smp = pl.program_id(0)
    blk = pl.program_id(1)
    rows = A_HEADS * 8
    hn = A_HEADS * A_NOPE

    step = smp * n_blocks + blk
    slot = step % 2

    def page_copies(s_, b_, slot_):
        out = []
        for i in range(ppb):
            page = pt_ref[s_, b_ * ppb + i]
            out.append(pltpu.make_async_copy(lat_hbm.at[la, page], lat_buf.at[slot_, pl.ds(i * PAGE, PAGE)],
                                             sem.at[slot_]))
            out.append(pltpu.make_async_copy(kr_hbm.at[la, page], kr_buf.at[slot_, :, pl.ds(i * PAGE, PAGE)],
                                             sem.at[slot_]))
        return out

    @pl.when(step == 0)
    def _():
        for cp in page_copies(smp, blk, slot):
            cp.start()

    @pl.when(step + 1 < pl.num_programs(0) * n_blocks)
    def _():
        wrap = blk + 1 == n_blocks
        for cp in page_copies(jnp.where(wrap, smp + 1, smp), jnp.where(wrap, 0, blk + 1), 1 - slot):
            cp.start()

    for cp in page_copies(smp, blk, slot):
        cp.wait()

    @pl.when(blk == 0)
    def _():
        m_ref[...] = jnp.full(m_ref.shape, -jnp.inf, F32)
        l_ref[...] = jnp.zeros(l_ref.shape, F32)
        acc_ref[...] = jnp.zeros(acc_ref.shape, F32)
        gk = gk_ref[...]
        for hd in range(A_HEADS):
            qs = q_ref[0, :, hd * SLOT:(hd + 1) * SLOT] * gk
            wq_ref[hd * A_NOPE:(hd + 1) * A_NOPE, :] = wukt_ref[hd]
            wq_ref[hn + hd * 8:hn + (hd + 1) * 8, :] = _dot(qs[:, :A_NOPE].astype(BF16), wukt_ref[hd]).astype(BF16)
            qr_ref[hd * 8:(hd + 1) * 8, :] = qs[:, A_NOPE:A_QK].astype(BF16)

    def scores(cb, krt):
        nk = cb.shape[0]
        res = _dot_nt(wq_ref[...], cb)
        ssq = []
        for hd in range(A_HEADS):
            kh = res[hd * A_NOPE:(hd + 1) * A_NOPE]
            ssq.append(jnp.broadcast_to(jnp.sum(kh * kh, 0, keepdims=True), (8, nk)))
        ssq = jnp.concatenate(ssq, 0) + jnp.sum(krt * krt, 0, keepdims=True)
        s = res[hn:] + _dot(qr_ref[...], krt.astype(BF16))
        return s * lax.rsqrt(ssq * (1.0 / A_QK) + EPS)

    def block(cbs, krts, new):
        sts = [scores(cb, krt) for cb, krt in zip(cbs, krts)]
        if new:
            nk = cbs[0].shape[0]
            tok = lax.broadcasted_iota(jnp.int32, (rows, nk), 0) & 7
            key = lax.broadcasted_iota(jnp.int32, (rows, nk), 1)
            sts = [jnp.where(key <= tok, st, -jnp.inf) for st in sts]
        ms = [jnp.max(st, -1, keepdims=True) for st in sts]
        prs = [jnp.exp(st - m) for st, m in zip(sts, ms)]
        ls = [jnp.sum(pr, -1, keepdims=True) for pr in prs]
        pvs = [_dot(pr.astype(BF16), cb) for pr, cb in zip(prs, cbs)]
        m_prev = m_ref[...]
        m_new = m_prev
        for m in ms:
            m_new = jnp.maximum(m_new, m)
        alpha = jnp.exp(m_prev - m_new)
        l_new = alpha * l_ref[...]
        acc = jnp.concatenate([alpha, alpha], 1) * acc_ref[...]
        for m, l, pv in zip(ms, ls, pvs):
            wgt = jnp.exp(m - m_new)
            l_new = l_new + wgt * l
            acc = acc + jnp.concatenate([wgt, wgt], 1) * pv
        m_ref[...] = m_new
        l_ref[...] = l_new
        acc_ref[...] = acc

    sub = max(ppb // PAGED_PARTS, 1) * PAGE
    block([lat_buf[slot, i:i + sub].astype(BF16) for i in range(0, ppb * PAGE, sub)],
          [kr_buf[slot, :, i:i + sub] for i in range(0, ppb * PAGE, sub)], False)

    @pl.when(blk == n_blocks - 1)
    def _():
        block([cn_ref[0].astype(BF16)], [kn_ref[0]], True)
        o_lat = (acc_ref[...] * (1.0 / l_ref[...][:, :1])).astype(BF16)
        for hd in range(A_HEADS):
            o_ref[0, :, hd * A_V:(hd + 1) * A_V] = _dot(o_lat[hd * 8:(hd + 1) * 8], wuv_ref[hd])


def _paged_attention(q, c_new, kpe_new_t, cache_latent, cache_krope_t, la, page_table, w, *, ppb):
    n = q.shape[0]
    n_pages = page_table.shape[1]
    ppb = min(ppb, n_pages)
    n_blocks = n_pages // ppb
    full = lambda a: pl.BlockSpec(a.shape, lambda i, j, pt: (0,) * a.ndim)
    rows = A_HEADS * 8
    params = [w["g_k"], w["w_ukt"], w["w_uv_h"]]

    hbm = pl.BlockSpec(memory_space=pl.ANY)
    grid_spec = pltpu.PrefetchScalarGridSpec(
        num_scalar_prefetch=1,
        grid=(n, n_blocks),
        in_specs=[
            pl.BlockSpec((1, 8, A_QPAD), lambda s, j, pt: (s, 0, 0)),
            pl.BlockSpec((1, PAGE, A_KV_LORA), lambda s, j, pt: (s, 0, 0)),
            pl.BlockSpec((1, A_ROPE, PAGE), lambda s, j, pt: (s, 0, 0)),
        ] + [full(p) for p in params] + [hbm, hbm],
        out_specs=pl.BlockSpec((1, 8, A_WIDTH), lambda s, j, pt: (s, 0, 0)),
        scratch_shapes=[
            pltpu.VMEM((A_HEADS * A_NOPE + rows, A_KV_LORA), BF16),
            pltpu.VMEM((rows, A_ROPE), BF16),
            pltpu.VMEM((rows, SLOT), F32),
            pltpu.VMEM((rows, SLOT), F32),
            pltpu.VMEM((rows, A_KV_LORA), F32),
            pltpu.VMEM((2, ppb * PAGE, A_KV_LORA), F32),
            pltpu.VMEM((2, A_ROPE, ppb * PAGE), F32),
            pltpu.SemaphoreType.DMA((2,)),
        ],
    )
    return pl.pallas_call(
        functools.partial(_paged_kernel, la=la, ppb=ppb, n_blocks=n_blocks),
        grid_spec=grid_spec,
        out_shape=jax.ShapeDtypeStruct((n, 8, A_WIDTH), F32),
        compiler_params=_cparams(("arbitrary", "arbitrary")),
        name="mla_paged",
    )(page_table, q, c_new, kpe_new_t, *params, cache_latent, cache_krope_t)


def _gated_out_kernel(x_ref, o_ref, z_ref, w_ref, y_ref):
    gated = o_ref[...] * _silu(z_ref[...])
    y_ref[...] = x_ref[...] + _dot(gated.astype(BF16), w_ref[...])


def _gated_out(x, o, z, w_o, *, tm):
    r = x.shape[0]
    tm = min(tm, r)
    row = lambda n: pl.BlockSpec((tm, n), lambda i: (i, 0))
    return pl.pallas_call(
        _gated_out_kernel,
        grid=(r // tm,),
        in_specs=[row(D_MODEL), row(o.shape[1]), row(z.shape[1]), pl.BlockSpec(w_o.shape, lambda i: (0, 0))],
        out_specs=row(D_MODEL),
        out_shape=jax.ShapeDtypeStruct((r, D_MODEL), F32),
        compiler_params=_cparams(("parallel",)),
        name="gated_out",
    )(x, o, z, w_o)


def _gdn_proj_kernel(x_ref, gn_ref, w_ref, wabt_ref, qkv_out, z_out, ab_out, abt_out):
    hb = _rms(x_ref[...], gn_ref[...]).astype(BF16)
    proj = _dot(hb, w_ref[...])
    o1 = B_CONV_CH
    o2 = o1 + B_WIDTH
    qkv_out[...] = proj[:, :o1]
    z_out[...] = proj[:, o1:o2]
    ab_out[...] = proj[:, o2:o2 + 2 * B_HEADS]
    abt_out[...] = _dot_nt(wabt_ref[...], hb)


def _gdn_project(x, w, *, tm):
    r = x.shape[0]
    tm = min(tm, r)
    row = lambda n: pl.BlockSpec((tm, n), lambda i: (i, 0))
    full = lambda a: pl.BlockSpec(a.shape, lambda i: (0,) * a.ndim)
    params = [w["b_norm"], w["w_in"], w["w_abt"]]
    return pl.pallas_call(
        _gdn_proj_kernel,
        grid=(r // tm,),
        in_specs=[row(D_MODEL)] + [full(p) for p in params],
        out_specs=[row(B_CONV_CH), row(B_WIDTH), row(2 * B_HEADS), pl.BlockSpec((2 * B_HEADS, tm), lambda i: (0, i))],
        out_shape=[
            jax.ShapeDtypeStruct((r, B_CONV_CH), F32),
            jax.ShapeDtypeStruct((r, B_WIDTH), F32),
            jax.ShapeDtypeStruct((r, 2 * B_HEADS), F32),
            jax.ShapeDtypeStruct((2 * B_HEADS, r), F32),
        ],
        compiler_params=_cparams(("parallel",)),
        name="gdn_proj",
    )(x, *params)


def _gdn_chunk_kernel(qkv_ref, ab_ref, abt_ref, conv0_ref, st0_ref, wc_ref, alr_ref, dtr_ref, alc_ref, dtc_ref,
                      go_ref, exp_ref, tri_ref, triu_ref, off_ref, avg_ref, o_ref, st_out, ext_ref, feat_ref, gcx_ref,
                      bx_ref, st_ref, lm_ref, inv_ref, t1_ref, in_ref, *, t, valid, bb):
    si = pl.program_id(1)
    hk = B_HEADS * B_DK
    units = [(bi, hd) for bi in range(bb) for hd in range(B_HEADS)]
    qsl = lambda hd: slice(hd * B_DK, (hd + 1) * B_DK)
    ksl = lambda hd: slice(hk + hd * B_DK, hk + (hd + 1) * B_DK)
    vsl = lambda hd: slice(2 * hk + hd * B_DV, 2 * hk + (hd + 1) * B_DV)

    @pl.when(si == 0)
    def _():
        ext_ref[:, 0:HALO] = conv0_ref[...]
        st_ref[...] = st0_ref[...]

    gcts = []
    for bi in range(bb):
        ext_ref[bi, HALO:HALO + t] = qkv_ref[bi]
        conv = ext_ref[bi, HALO - 3:HALO - 3 + t] * wc_ref[0:1]
        for j in range(1, B_CONV):
            conv = conv + ext_ref[bi, HALO - 3 + j:HALO - 3 + j + t] * wc_ref[j:j + 1]
        feat_ref[bi] = _silu(conv)
        ext_ref[bi, 0:HALO] = ext_ref[bi, t:t + HALO]

        ab = ab_ref[bi]
        g = -jnp.exp(alr_ref[...]) * _softplus(ab[:, :B_HEADS] + dtr_ref[...])
        beta = jax.nn.sigmoid(ab[:, B_HEADS:])
        if valid < t:
            live = lax.broadcasted_iota(jnp.int32, (t, B_HEADS), 0) < valid
            g = jnp.where(live, g, 0.0)
            beta = jnp.where(live, beta, 0.0)
        bx_ref[bi] = _dot_sel(exp_ref[...], beta, sel_first=False)
        gcx_ref[bi] = _dot_sel(tri_ref[...], _dot_sel(exp_ref[...], g, sel_first=False), sel_first=True)
        gt = -jnp.exp(alc_ref[...]) * _softplus(abt_ref[bi, 0][:B_HEADS] + dtc_ref[...])
        if valid < t:
            gt = jnp.where(lax.broadcasted_iota(jnp.int32, (B_HEADS, t), 1) < valid, gt, 0.0)
        gcts.append(_dot_sel(triu_ref[...], gt, sel_first=False))

    ii = lax.broadcasted_iota(jnp.int32, (t, t), 0)
    jj = lax.broadcasted_iota(jnp.int32, (t, t), 1)
    incl = ii >= jj
    strict = ii > jj
    eye = (ii == jj).astype(F32)
    pair = (ii >> 1) == (jj >> 1)

    def cols(x):
        return x[:, :t] if t <= B_DK else jnp.tile(x, (1, t // B_DK))

    for u, (bi, hd) in enumerate(units):
        qh = feat_ref[bi, :, qsl(hd)]
        kh = feat_ref[bi, :, ksl(hd)]
        qh = qh * lax.rsqrt(jnp.sum(qh * qh, -1, keepdims=True) + EPS) * (B_DK ** -0.5)
        kh = kh * lax.rsqrt(jnp.sum(kh * kh, -1, keepdims=True) + EPS)
        feat_ref[bi, :, qsl(hd)] = qh
        feat_ref[bi, :, ksl(hd)] = kh
        decay = jnp.exp(jnp.where(incl, cols(gcx_ref[bi, :, qsl(hd)]) - gcts[bi][hd:hd + 1, :], -jnp.inf))
        khb = kh.astype(BF16)
        lmat = jnp.where(strict, _dot_nt((kh * bx_ref[bi, :, qsl(hd)]).astype(BF16), khb) * decay, 0.0)
        lm_ref[u] = lmat.astype(BF16)
        in_ref[u] = (_dot_nt(qh.astype(BF16), khb) * decay).astype(BF16)
        inv_ref[u] = (eye - jnp.where(pair, lmat, 0.0)).astype(BF16)

    for lv in range(off_ref.shape[0]):
        for u in range(len(units)):
            t1_ref[u] = _dot(lm_ref[u] * off_ref[lv], inv_ref[u]).astype(BF16)
        for u in range(len(units)):
            inv_ref[u] = inv_ref[u] - _dot(inv_ref[u], t1_ref[u]).astype(BF16)

    go = go_ref[...]
    us, ws, egs = [], [], []
    for u, (bi, hd) in enumerate(units):
        invb = inv_ref[u]
        bc = bx_ref[bi, :, qsl(hd)]
        eg = jnp.exp(gcx_ref[bi, :, qsl(hd)])
        kb = feat_ref[bi, :, ksl(hd)] * bc
        us.append(_dot(invb, (feat_ref[bi, :, vsl(hd)] * bc).astype(BF16)))
        ws.append(_dot(invb, (kb * eg).astype(BF16)).astype(BF16))
        egs.append(eg)
    vns = []
    for u, (bi, hd) in enumerate(units):
        vns.append((us[u] - _dot(ws[u], st_ref[bi, hd].astype(BF16))).astype(BF16))
    for u, (bi, hd) in enumerate(units):
        st = st_ref[bi, hd]
        o = _dot((feat_ref[bi, :, qsl(hd)] * egs[u]).astype(BF16), st.astype(BF16)) + _dot(in_ref[u], vns[u])
        gcc = gcx_ref[bi, :, qsl(hd)]
        glast = gcc[t - 1:t, :]
        kdec = feat_ref[bi, :, ksl(hd)] * jnp.exp(glast - gcc)
        st_ref[bi, hd] = st * jnp.exp(glast) + _dot_tn(kdec.astype(BF16), vns[u])
        if t >= B_DV:
            o2 = o * o
            o2h = o2.astype(BF16)
            ms = _dot(o2h, avg_ref[...]) + _dot((o2 - o2h.astype(F32)).astype(BF16), avg_ref[...])
        else:
            ms = jnp.mean(o * o, -1, keepdims=True)
        o_ref[bi, :, qsl(hd)] = o * lax.rsqrt(ms + EPS) * go

    @pl.when(si == pl.num_programs(1) - 1)
    def _():
        st_out[...] = st_ref[...]


def _gdn_chunked(qkv, ab, abt, conv0, st0, w, *, t, valid, bb):
    b, s, _ = qkv.shape
    t = min(t, s)
    full = lambda a: pl.BlockSpec(a.shape, lambda i, j: (0,) * a.ndim)
    expand = jnp.repeat(jnp.eye(B_HEADS, dtype=BF16), B_DV, axis=1)
    pos = jnp.arange(t)
    tri = (pos[:, None] >= pos[None, :]).astype(BF16)
    triu = (pos[:, None] <= pos[None, :]).astype(BF16)
    blk = lambda lg: pos >> lg
    off = jnp.stack([((blk(lg)[:, None] - blk(lg)[None, :] == 1) & ((blk(lg)[:, None] & 1) == 1)).astype(BF16)
                     for lg in range(1, t.bit_length() - 1)])
    avg = jnp.full((B_DV, B_DV), 1.0 / B_DV, BF16)
    params = [w["w_conv"], w["a_log_r"], w["dt_r"], w["a_log_c"], w["dt_c"], w["g_o"], expand, tri, triu, off, avg]
    bb = min(bb, b)
    assert b % bb == 0 and s % t == 0
    return pl.pallas_call(
        functools.partial(_gdn_chunk_kernel, t=t, valid=valid, bb=bb),
        grid=(b // bb, s // t),
        in_specs=[
            pl.BlockSpec((bb, t, B_CONV_CH), lambda i, j: (i, j, 0)),
            pl.BlockSpec((bb, t, 2 * B_HEADS), lambda i, j: (i, j, 0)),
            pl.BlockSpec((bb, 1, 2 * B_HEADS, t), lambda i, j: (i, j, 0, 0)),
            pl.BlockSpec((bb, HALO, B_CONV_CH), lambda i, j: (i, 0, 0)),
            pl.BlockSpec((bb, B_HEADS, B_DK, B_DV), lambda i, j: (i, 0, 0, 0)),
        ] + [full(p) for p in params],
        out_specs=[
            pl.BlockSpec((bb, t, B_WIDTH), lambda i, j: (i, j, 0)),
            pl.BlockSpec((bb, B_HEADS, B_DK, B_DV), lambda i, j: (i, 0, 0, 0)),
        ],
        out_shape=[
            jax.ShapeDtypeStruct((b, s, B_WIDTH), F32),
            jax.ShapeDtypeStruct((b, B_HEADS, B_DK, B_DV), F32),
        ],
        scratch_shapes=[
            pltpu.VMEM((bb, t + HALO, B_CONV_CH), F32),
            pltpu.VMEM((bb, t, B_CONV_CH), F32),
            pltpu.VMEM((bb, t, B_WIDTH), F32),
            pltpu.VMEM((bb, t, B_WIDTH), F32),
            pltpu.VMEM((bb, B_HEADS, B_DK, B_DV), F32),
            pltpu.VMEM((bb * B_HEADS, t, t), BF16),
            pltpu.VMEM((bb * B_HEADS, t, t), BF16),
            pltpu.VMEM((bb * B_HEADS, t, t), BF16),
            pltpu.VMEM((bb * B_HEADS, t, t), BF16),
        ],
        compiler_params=_cparams(("parallel", "arbitrary")),
        name="gdn_chunk",
    )(qkv, ab, abt, conv0, st0, *params)


def _pad_heads(wm, n_in):
    d = wm.shape[-1]
    return jnp.pad(wm, ((0, 0), (0, 0), (0, SLOT - d))).reshape(n_in, A_HEADS * SLOT)


def _mla_weights(a_norm, a_w_in, a_g_qa, a_w_uq, a_g_kv, a_w_uk, a_w_uv, a_g_q, a_g_k, a_w_o):
    o1 = A_Q_LORA
    o2 = o1 + A_KV_LORA
    o3 = o2 + A_ROPE
    kslot = jnp.pad(a_w_in[:, o2:o3], ((0, 0), (A_NOPE, SLOT - A_QK)))
    w_in = jnp.concatenate([a_w_in[:, :o2], a_w_in[:, o3:], kslot], 1).astype(BF16)
    pad_gain = lambda g: jnp.pad(g, (0, SLOT - A_QK)).reshape(1, SLOT)
    return {
        "a_norm": a_norm.reshape(1, D_MODEL),
        "w_in": w_in,
        "g_qa": a_g_qa.reshape(1, A_Q_LORA),
        "w_uq": _pad_heads(a_w_uq.reshape(A_Q_LORA, A_HEADS, A_QK), A_Q_LORA).astype(BF16),
        "g_kv": a_g_kv.reshape(1, A_KV_LORA),
        "g_q": pad_gain(a_g_q * (A_QK ** -0.5)),
        "g_k": pad_gain(a_g_k),
        "w_uk": _pad_heads(a_w_uk, A_KV_LORA).astype(BF16),
        "w_ukt": jnp.transpose(a_w_uk, (1, 2, 0)).astype(BF16),
        "w_uv": a_w_uv.reshape(A_KV_LORA, A_WIDTH).astype(BF16),
        "w_uv_h": jnp.transpose(a_w_uv, (1, 0, 2)).astype(BF16),
        "w_o": a_w_o.astype(BF16),
    }


def _gdn_weights(b_norm, b_w_in, b_w_conv, b_a_log, b_dt_bias, b_g_o, b_w_o):
    o2 = B_CONV_CH + B_WIDTH
    w_in = jnp.pad(b_w_in, ((0, 0), (0, SLOT - 2 * B_HEADS))).astype(BF16)
    return {
        "b_norm": b_norm.reshape(1, D_MODEL),
        "w_in": w_in,
        "w_abt": b_w_in[:, o2:].T.astype(BF16),
        "w_conv": b_w_conv,
        "a_log_r": b_a_log.reshape(1, B_HEADS),
        "dt_r": b_dt_bias.reshape(1, B_HEADS),
        "a_log_c": b_a_log.reshape(B_HEADS, 1),
        "dt_c": b_dt_bias.reshape(B_HEADS, 1),
        "g_o": b_g_o.reshape(1, B_DV),
        "w_o": b_w_o.astype(BF16),
    }


def _mla_layer(xp, xs, cache_latent, cache_krope, la, page_table, w, tabs_p, tabs_s):
    b, s, _ = xp.shape
    n, t, _ = xs.shape
    q, c_p, kpe_p, z, k, v = _mla_project(xp, tabs_p, w, with_kv=True, tm=256, q_dtype=BF16)
    bound = A_QK * BF16_NORM_SLACK * jnp.max(jnp.abs(w["g_q"])) * jnp.max(jnp.abs(w["g_k"]))
    o = _flash_attention(q, k, v, bound, tq=512)
    yp = _gated_out(xp.reshape(b * s, D_MODEL), o.reshape(b * s, A_WIDTH), z.reshape(b * s, A_WIDTH), w["w_o"], tm=512)

    q, c_s, kpe_s, z = _mla_project(xs.reshape(1, n * t, D_MODEL), tabs_s, w, with_kv=False, tm=256, q_dtype=F32)
    c_s = c_s.reshape(n, t, A_KV_LORA)
    kpe_s = kpe_s.reshape(n, t, A_ROPE)
    q8 = jnp.pad(q.reshape(n, t, A_QPAD), ((0, 0), (0, 8 - t), (0, 0)))
    c_new = jnp.pad(c_s, ((0, 0), (0, PAGE - t), (0, 0)))
    kpe_new_t = jnp.pad(jnp.swapaxes(kpe_s, 1, 2), ((0, 0), (0, 0), (0, PAGE - t)))
    o = _paged_attention(q8, c_new, kpe_new_t, cache_latent, jnp.swapaxes(cache_krope, 2, 3), la, page_table, w,
                         ppb=16)
    ys = _gated_out(xs.reshape(n * t, D_MODEL), o[:, :t].reshape(n * t, A_WIDTH), z.reshape(n * t, A_WIDTH),
                    w["w_o"], tm=512)
    return yp.reshape(b, s, D_MODEL), ys.reshape(n, t, D_MODEL), c_p, kpe_p, c_s, kpe_s


def _gdn_layer(xp, xs, state_conv, state_ssm, w):
    b, s, _ = xp.shape
    n, t, _ = xs.shape
    tc = min(256, s)
    qkv, z, ab, abt = _gdn_project(xp.reshape(b * s, D_MODEL), w, tm=256)
    qkv = qkv.reshape(b, s, B_CONV_CH)
    abt = abt.reshape(2 * B_HEADS, b, s // tc, tc).transpose(1, 2, 0, 3)
    o, st_p = _gdn_chunked(qkv, ab.reshape(b, s, 2 * B_HEADS), abt,
                           jnp.zeros((b, HALO, B_CONV_CH), F32), jnp.zeros((b, B_HEADS, B_DK, B_DV), F32),
                           w, t=tc, valid=tc, bb=1)
    yp = _gated_out(xp.reshape(b * s, D_MODEL), o.reshape(b * s, B_WIDTH), z, w["w_o"], tm=512)
    conv_p = qkv[:, s - (B_CONV - 1):]

    qkv, z, ab, abt = _gdn_project(xs.reshape(n * t, D_MODEL), w, tm=256)
    qkv = qkv.reshape(n, t, B_CONV_CH)
    pad_t = lambda a: jnp.pad(a, ((0, 0), (0, 8 - t), (0, 0)))
    abt = jnp.pad(abt.reshape(2 * B_HEADS, n, t).transpose(1, 0, 2), ((0, 0), (0, 0), (0, 8 - t)))
    conv0 = jnp.pad(state_conv, ((0, 0), (HALO - (B_CONV - 1), 0), (0, 0)))
    o, st_s = _gdn_chunked(pad_t(qkv), pad_t(ab.reshape(n, t, 2 * B_HEADS)), abt.reshape(n, 1, 2 * B_HEADS, 8),
                           conv0, state_ssm, w, t=8, valid=t, bb=4)
    ys = _gated_out(xs.reshape(n * t, D_MODEL), o[:, :t].reshape(n * t, B_WIDTH), z, w["w_o"], tm=512)
    conv_s = jnp.concatenate([state_conv, qkv], 1)[:, -(B_CONV - 1):]
    return yp.reshape(b, s, D_MODEL), ys.reshape(n, t, D_MODEL), conv_p, st_p, conv_s, st_s


def kernel(x_prompt, x_sample, cache_latent, cache_krope, page_table, state_conv, state_ssm,
           a_norm, a_w_in, a_g_qa, a_w_uq, a_g_kv, a_w_uk, a_w_uv, a_g_q, a_g_k, a_w_o,
           b_norm, b_w_in, b_w_conv, b_a_log, b_dt_bias, b_g_o, b_w_o):
    s = x_prompt.shape[1]
    n, t, _ = x_sample.shape
    past = page_table.shape[1] * PAGE
    p_pad = -(-(s + t) // SLOT) * SLOT
    pos = jnp.concatenate([jnp.arange(s), past + jnp.arange(t), jnp.zeros((p_pad - s - t,), jnp.int32)]).astype(F32)
    tabs = _rope_tables(pos)
    tabs_p = tuple(tb[:s] for tb in tabs)
    tabs_s = tuple(jnp.tile(tb[s:s + t], (n, 1)) for tb in tabs)

    wa = _mla_weights(a_norm[0], a_w_in[0], a_g_qa[0], a_w_uq[0], a_g_kv[0], a_w_uk[0], a_w_uv[0], a_g_q[0],
                      a_g_k[0], a_w_o[0])
    wb = _gdn_weights(b_norm[0], b_w_in[0], b_w_conv[0], b_a_log[0], b_dt_bias[0], b_g_o[0], b_w_o[0])

    xp, xs, lat_p, kpe_p, lat_s, kpe_s = _mla_layer(x_prompt, x_sample, cache_latent, cache_krope, 0, page_table,
                                                    wa, tabs_p, tabs_s)
    xp, xs, conv_p, ssm_p, conv_s, ssm_s = _gdn_layer(xp, xs, state_conv[0], state_ssm[0], wb)
    return (xp, xs, lat_p[None], kpe_p[None], lat_s[None], kpe_s[None],
            conv_p[None], ssm_p[None], conv_s[None], ssm_s[None])
```

```python
import functools

import jax
import jax.numpy as jnp
from jax import lax
from jax.experimental import pallas as pl
from jax.experimental.pallas import tpu as pltpu

F32 = jnp.float32
BF16 = jnp.bfloat16
EPS = 1e-6

D_MODEL = 1024
PAGE = 128
A_HEADS = 8
A_NOPE = 64
A_ROPE = 32
A_QK = A_NOPE + A_ROPE
A_V = 64
A_Q_LORA = 384
A_KV_LORA = 256
A_WIDTH = A_HEADS * A_V
ROPE_THETA = 10000.0
SLOT = 128
A_QPAD = A_HEADS * SLOT
B_HEADS = 8
B_DK = 64
B_DV = 64
B_WIDTH = B_HEADS * B_DV
B_CONV = 4
B_CONV_CH = 2 * B_HEADS * B_DK + B_WIDTH
HALO = 8

VMEM_LIMIT = 56 * 1024 * 1024
FIXED_SHIFT_MAX = 30.0
BF16_NORM_SLACK = 1.02
PAGED_SLOTS = 3
PAGED_PARTS = 2


def _cparams(sem):
    return pltpu.CompilerParams(dimension_semantics=sem, vmem_limit_bytes=VMEM_LIMIT)


def _dot(a, b):
    return jnp.dot(a, b, preferred_element_type=F32)


def _dot_nt(a, b):
    return lax.dot_general(a, b, (((1,), (1,)), ((), ())), preferred_element_type=F32)


def _dot_tn(a, b):
    return lax.dot_general(a, b, (((0,), (0,)), ((), ())), preferred_element_type=F32)


def _dot_sel(sel_bf16, x, *, sel_first):
    x1 = x.astype(BF16)
    r1 = x - x1.astype(F32)
    x2 = r1.astype(BF16)
    x3 = (r1 - x2.astype(F32)).astype(BF16)
    if sel_first:
        return _dot(sel_bf16, x1) + _dot(sel_bf16, x2) + _dot(sel_bf16, x3)
    return _dot(x1, sel_bf16) + _dot(x2, sel_bf16) + _dot(x3, sel_bf16)


def _rms(x, g):
    return x * lax.rsqrt(jnp.mean(x * x, -1, keepdims=True) + EPS) * g


def _silu(x):
    return x * jax.nn.sigmoid(x)


def _softplus(x):
    return jnp.maximum(x, 0.0) + jnp.log(1.0 + jnp.exp(-jnp.abs(x)))


def _rope_table_kernel(pos_ref, inv_ref, cos_ref, sin_ref):
    ang = inv_ref[...] * pos_ref[...]
    cos_ref[...] = jnp.cos(ang)
    sin_ref[...] = jnp.sin(ang)


def _rope_tables(pos):
    half = A_ROPE // 2
    p = pos.shape[0]
    inv = (ROPE_THETA ** (-jnp.arange(half, dtype=F32) / half)).reshape(half, 1)
    cos_t, sin_t = pl.pallas_call(
        _rope_table_kernel,
        out_shape=(jax.ShapeDtypeStruct((half, p), F32),) * 2,
        name="rope_tables",
    )(pos.reshape(1, p), inv)
    cos = cos_t.T
    sin = sin_t.T
    one = jnp.ones((p, A_NOPE), F32)
    zn = jnp.zeros((p, A_NOPE), F32)
    zh = jnp.zeros((p, half), F32)
    zp = jnp.zeros((p, SLOT - A_QK), F32)
    tc = jnp.concatenate([one, cos, cos, zp], 1)
    ts1 = jnp.concatenate([zn, -sin, zh, zp], 1)
    ts2 = jnp.concatenate([zn, zh, sin, zp], 1)
    return tc, ts1, ts2


def _mla_proj_kernel(*refs, with_kv):
    (x_ref, tc_ref, ts1_ref, ts2_ref, gn_ref, win_ref, gqa_ref, wuq_ref, gkv_ref, gq_ref, ones_ref) = refs[:11]
    if with_kv:
        wuk_ref, gk_ref, wuv_ref = refs[11:14]
        q_out, c_out, kpe_out, z_out, k_out, v_out = refs[14:]
    else:
        q_out, c_out, kpe_out, z_out = refs[11:]
    x = x_ref[0]
    h = _rms(x, gn_ref[...])
    proj = _dot(h.astype(BF16), win_ref[...])
    o1 = A_Q_LORA
    o2 = o1 + A_KV_LORA
    o3 = o2 + A_WIDTH
    qa = _rms(proj[:, :o1], gqa_ref[...])
    c = _rms(proj[:, o1:o2], gkv_ref[...])
    z_out[0] = proj[:, o2:o3]
    c_out[0] = c
    q = _dot(qa.astype(BF16), wuq_ref[...])
    tc = tc_ref[...]
    ts1 = ts1_ref[...]
    ts2 = ts2_ref[...]

    def rope(s):
        return s * tc + pltpu.roll(s, SLOT - A_ROPE // 2, 1) * ts1 + pltpu.roll(s, A_ROPE // 2, 1) * ts2

    ones = ones_ref[...]

    def head_norm(s, g):
        s2 = s * s
        s2h = s2.astype(BF16)
        ms = (_dot(s2h, ones) + _dot((s2 - s2h.astype(F32)).astype(BF16), ones)) * (1.0 / A_QK)
        return s * lax.rsqrt(ms + EPS) * g

    kslot = rope(proj[:, o3:o3 + SLOT])
    kpe_out[0] = kslot[:, A_NOPE:A_QK]
    gq = gq_ref[...]
    for hd in range(A_HEADS):
        sl = slice(hd * SLOT, (hd + 1) * SLOT)
        q_out[0, :, sl] = head_norm(rope(q[:, sl]), gq).astype(q_out.dtype)
    if with_kv:
        cb = c.astype(BF16)
        kn = _dot(cb, wuk_ref[...])
        v_out[0] = _dot(cb, wuv_ref[...]).astype(v_out.dtype)
        gk = gk_ref[...]
        for hd in range(A_HEADS):
            sl = slice(hd * SLOT, (hd + 1) * SLOT)
            k_out[0, :, sl] = head_norm(kn[:, sl] + kslot, gk).astype(k_out.dtype)


def _mla_project(x, tabs, w, *, with_kv, tm, q_dtype):
    b, s, _ = x.shape
    tm = min(tm, s)
    grid = (s // tm, b)
    full = lambda a: pl.BlockSpec(a.shape, lambda i, j: (0,) * a.ndim)
    row = lambda n: pl.BlockSpec((1, tm, n), lambda i, j: (j, i, 0))
    tab = pl.BlockSpec((tm, SLOT), lambda i, j: (i, 0))
    params = [w["a_norm"], w["w_in"], w["g_qa"], w["w_uq"], w["g_kv"], w["g_q"], jnp.ones((SLOT, SLOT), BF16)]
    out_shape = [
        jax.ShapeDtypeStruct((b, s, A_QPAD), q_dtype),
        jax.ShapeDtypeStruct((b, s, A_KV_LORA), F32),
        jax.ShapeDtypeStruct((b, s, A_ROPE), F32),
        jax.ShapeDtypeStruct((b, s, A_WIDTH), F32),
    ]
    out_specs = [row(A_QPAD), row(A_KV_LORA), row(A_ROPE), row(A_WIDTH)]
    if with_kv:
        params += [w["w_uk"], w["g_k"], w["w_uv"]]
        out_shape += [jax.ShapeDtypeStruct((b, s, A_QPAD), BF16), jax.ShapeDtypeStruct((b, s, A_WIDTH), BF16)]
        out_specs += [row(A_QPAD), row(A_WIDTH)]
    return pl.pallas_call(
        functools.partial(_mla_proj_kernel, with_kv=with_kv),
        grid=grid,
        in_specs=[row(D_MODEL), tab, tab, tab] + [full(p) for p in params],
        out_specs=out_specs,
        out_shape=out_shape,
        compiler_params=_cparams(("parallel", "parallel")),
        name="mla_proj_kv" if with_kv else "mla_proj",
    )(x, *tabs, *params)


def _flash_kernel(qi_ref, ki_ref, dv_ref, fx_ref, q_ref, k_ref, v_ref, sh_ref, o_ref, m_ref, l_ref, acc_ref,
                  *, tq, tk):
    del qi_ref
    p = pl.program_id(1)
    ki = ki_ref[p]
    dv = dv_ref[p]
    fixed = fx_ref[0] == 1

    @pl.when(ki == 0)
    def _():
        m_ref[...] = jnp.full(m_ref.shape, -jnp.inf, F32)
        l_ref[...] = jnp.zeros(l_ref.shape, F32)
        acc_ref[...] = jnp.zeros(acc_ref.shape, F32)

    low = lax.broadcasted_iota(jnp.int32, (tq, SLOT), 1) < A_V

    def step(d, fixed_shift):
        diag = d is not None
        if diag:
            keep = (lax.broadcasted_iota(jnp.int32, (tq, tk), 1) + d * tk
                    <= lax.broadcasted_iota(jnp.int32, (tq, tk), 0))
        if fixed_shift:
            shift = sh_ref[:, :1]
        for j in range(A_HEADS // 2):
            pv = []
            al = []
            vpair = v_ref[0, :, j * SLOT:(j + 1) * SLOT]
            for e in range(2):
                hd = 2 * j + e
                sl = slice(hd * SLOT, (hd + 1) * SLOT)
                s = _dot_nt(q_ref[0, :, sl], k_ref[0, :, sl])
                if diag:
                    s = jnp.where(keep, s, -jnp.inf)
                if fixed_shift:
                    pr = jnp.exp(s - shift)
                    l_ref[hd] = l_ref[hd] + jnp.sum(pr, -1, keepdims=True)
                else:
                    m_prev = m_ref[hd]
                    m_new = jnp.maximum(m_prev, jnp.max(s, -1, keepdims=True))
                    alpha = jnp.exp(m_prev - m_new)
                    pr = jnp.exp(s - m_new[:, :1])
                    l_ref[hd] = alpha * l_ref[hd] + jnp.sum(pr, -1, keepdims=True)
                    m_ref[hd] = m_new
                    al.append(alpha)
                pv.append(_dot(pr.astype(BF16), vpair))
            sl = slice(j * SLOT, (j + 1) * SLOT)
            if fixed_shift:
                acc_ref[:, sl] = acc_ref[:, sl] + jnp.where(low, pv[0], pv[1])
            else:
                acc_ref[:, sl] = jnp.where(low, al[0], al[1]) * acc_ref[:, sl] + jnp.where(low, pv[0], pv[1])

    def finish():
        for j in range(A_HEADS // 2):
            sl = slice(j * SLOT, (j + 1) * SLOT)
            linv = jnp.where(low, 1.0 / l_ref[2 * j], 1.0 / l_ref[2 * j + 1])
            o_ref[0, :, sl] = acc_ref[:, sl] * linv

    n_diag = tq // tk
    for fs in (True, False):
        mode = fixed if fs else jnp.logical_not(fixed)

        @pl.when(jnp.logical_and(mode, dv < 0))
        def _():
            step(None, fs)

        for d in range(n_diag):
            @pl.when(jnp.logical_and(mode, dv == d))
            def _():
                step(d, fs)
                if d == n_diag - 1:
                    finish()


def _flash_attention(q, k, v, bound, *, tq, tk):
    b, s, _ = q.shape
    tq = min(tq, s)
    tk = min(tk, tq)
    assert s % tq == 0 and tq % tk == 0
    nq = s // tq
    r = tq // tk
    pairs = [(i, j, j - r * i if j >= r * i else -1) for i in range(nq) for j in range(r * (i + 1))]
    qi = jnp.asarray([pr[0] for pr in pairs], jnp.int32)
    ki = jnp.asarray([pr[1] for pr in pairs], jnp.int32)
    dv = jnp.asarray([pr[2] for pr in pairs], jnp.int32)
    fx = (bound <= FIXED_SHIFT_MAX).astype(jnp.int32).reshape(1)
    sh = jnp.full((1, SLOT), bound, F32)
    grid_spec = pltpu.PrefetchScalarGridSpec(
        num_scalar_prefetch=4,
        grid=(b, len(pairs)),
        in_specs=[
            pl.BlockSpec((1, tq, A_QPAD), lambda bi, p, qt, kt, dt, fx: (bi, qt[p], 0)),
            pl.BlockSpec((1, tk, A_QPAD), lambda bi, p, qt, kt, dt, fx: (bi, kt[p], 0)),
            pl.BlockSpec((1, tk, A_WIDTH), lambda bi, p, qt, kt, dt, fx: (bi, kt[p], 0)),
            pl.BlockSpec((1, SLOT), lambda bi, p, qt, kt, dt, fx: (0, 0)),
        ],
        out_specs=pl.BlockSpec((1, tq, A_WIDTH), lambda bi, p, qt, kt, dt, fx: (bi, qt[p], 0)),
        scratch_shapes=[
            pltpu.VMEM((A_HEADS, tq, SLOT), F32),
            pltpu.VMEM((A_HEADS, tq, SLOT), F32),
            pltpu.VMEM((tq, A_WIDTH), F32),
        ],
    )
    return pl.pallas_call(
        functools.partial(_flash_kernel, tq=tq, tk=tk),
        grid_spec=grid_spec,
        out_shape=jax.ShapeDtypeStruct((b, s, A_WIDTH), F32),
        compiler_params=_cparams(("parallel", "arbitrary")),
        name="mla_flash",
    )(qi, ki, dv, fx, q, k, v, sh)


def _paged_kernel(pt_ref, q_ref, cn_ref, kn_ref, gk_ref, wukt_ref, wuv_ref, lat_hbm, kr_hbm, o_ref,
                  wq_ref, qr_ref, m_ref, l_ref, acc_ref, lat_buf, kr_buf, sem, *, la, ppb, n_blocks):
    smp = pl.program_id(0)
    blk = pl.program_id(1)
    rows = A_HEADS * 8
    hn = A_HEADS * A_NOPE

    step = smp * n_blocks + blk
    n_steps = pl.num_programs(0) * n_blocks
    ahead = PAGED_SLOTS - 1

    def page_copies(g):
        s_ = g // n_blocks
        b_ = g % n_blocks
        slot_ = g % PAGED_SLOTS
        out = []
        for i in range(ppb):
            page = pt_ref[s_, b_ * ppb + i]
            out.append(pltpu.make_async_copy(lat_hbm.at[la, page], lat_buf.at[slot_, pl.ds(i * PAGE, PAGE)],
                                             sem.at[slot_]))
            out.append(pltpu.make_async_copy(kr_hbm.at[la, page], kr_buf.at[slot_, :, pl.ds(i * PAGE, PAGE)],
                                             sem.at[slot_]))
        return out

    for d in range(ahead):
        @pl.when(jnp.logical_and(step == 0, d < n_steps))
        def _():
            for cp in page_copies(step + d):
                cp.start()

    @pl.when(step + ahead < n_steps)
    def _():
        for cp in page_copies(step + ahead):
            cp.start()

    for cp in page_copies(step):
        cp.wait()
    slot = step % PAGED_SLOTS

    @pl.when(blk == 0)
    def _():
        m_ref[...] = jnp.full(m_ref.shape, -jnp.inf, F32)
        l_ref[...] = jnp.zeros(l_ref.shape, F32)
        acc_ref[...] = jnp.zeros(acc_ref.shape, F32)
        gk = gk_ref[...]
        for hd in range(A_HEADS):
            qs = q_ref[0, :, hd * SLOT:(hd + 1) * SLOT] * gk
            wq_ref[hd * A_NOPE:(hd + 1) * A_NOPE, :] = wukt_ref[hd]
            wq_ref[hn + hd * 8:hn + (hd + 1) * 8, :] = _dot(qs[:, :A_NOPE].astype(BF16), wukt_ref[hd]).astype(BF16)
            qr_ref[hd * 8:(hd + 1) * 8, :] = qs[:, A_NOPE:A_QK].astype(BF16)

    def scores(cb, krt):
        nk = cb.shape[0]
        res = _dot_nt(wq_ref[...], cb)
        ssq = []
        for hd in range(A_HEADS):
            kh = res[hd * A_NOPE:(hd + 1) * A_NOPE]
            ssq.append(jnp.broadcast_to(jnp.sum(kh * kh, 0, keepdims=True), (8, nk)))
        ssq = jnp.concatenate(ssq, 0) + jnp.sum(krt * krt, 0, keepdims=True)
        s = res[hn:] + _dot(qr_ref[...], krt.astype(BF16))
        return s * lax.rsqrt(ssq * (1.0 / A_QK) + EPS)

    def block(cbs, krts, new):
        sts = [scores(cb, krt) for cb, krt in zip(cbs, krts)]
        if new:
            nk = cbs[0].shape[0]
            tok = lax.broadcasted_iota(jnp.int32, (rows, nk), 0) & 7
            key = lax.broadcasted_iota(jnp.int32, (rows, nk), 1)
            sts = [jnp.where(key <= tok, st, -jnp.inf) for st in sts]
        ms = [jnp.max(st, -1, keepdims=True) for st in sts]
        prs = [jnp.exp(st - m) for st, m in zip(sts, ms)]
        ls = [jnp.sum(pr, -1, keepdims=True) for pr in prs]
        pvs = [_dot(pr.astype(BF16), cb) for pr, cb in zip(prs, cbs)]
        m_prev = m_ref[...]
        m_new = m_prev
        for m in ms:
            m_new = jnp.maximum(m_new, m)
        alpha = jnp.exp(m_prev - m_new)
        l_new = alpha * l_ref[...]
        acc = jnp.concatenate([alpha, alpha], 1) * acc_ref[...]
        for m, l, pv in zip(ms, ls, pvs):
            wgt = jnp.exp(m - m_new)
            l_new = l_new + wgt * l
            acc = acc + jnp.concatenate([wgt, wgt], 1) * pv
        m_ref[...] = m_new
        l_ref[...] = l_new
        acc_ref[...] = acc

    sub = max(ppb // PAGED_PARTS, 1) * PAGE
    block([lat_buf[slot, i:i + sub].astype(BF16) for i in range(0, ppb * PAGE, sub)],
          [kr_buf[slot, :, i:i + sub] for i in range(0, ppb * PAGE, sub)], False)

    @pl.when(blk == n_blocks - 1)
    def _():
        block([cn_ref[0].astype(BF16)], [kn_ref[0]], True)
        o_lat = (acc_ref[...] * (1.0 / l_ref[...][:, :1])).astype(BF16)
        for hd in range(A_HEADS):
            o_ref[0, :, hd * A_V:(hd + 1) * A_V] = _dot(o_lat[hd * 8:(hd + 1) * 8], wuv_ref[hd])


def _paged_attention(q, c_new, kpe_new_t, cache_latent, cache_krope_t, la, page_table, w, *, ppb):
    n = q.shape[0]
    n_pages = page_table.shape[1]
    ppb = min(ppb, n_pages)
    n_blocks = n_pages // ppb
    full = lambda a: pl.BlockSpec(a.shape, lambda i, j, pt: (0,) * a.ndim)
    rows = A_HEADS * 8
    params = [w["g_k"], w["w_ukt"], w["w_uv_h"]]

    hbm = pl.BlockSpec(memory_space=pl.ANY)
    grid_spec = pltpu.PrefetchScalarGridSpec(
        num_scalar_prefetch=1,
        grid=(n, n_blocks),
        in_specs=[
            pl.BlockSpec((1, 8, A_QPAD), lambda s, j, pt: (s, 0, 0)),
            pl.BlockSpec((1, PAGE, A_KV_LORA), lambda s, j, pt: (s, 0, 0)),
            pl.BlockSpec((1, A_ROPE, PAGE), lambda s, j, pt: (s, 0, 0)),
        ] + [full(p) for p in params] + [hbm, hbm],
        out_specs=pl.BlockSpec((1, 8, A_WIDTH), lambda s, j, pt: (s, 0, 0)),
        scratch_shapes=[
            pltpu.VMEM((A_HEADS * A_NOPE + rows, A_KV_LORA), BF16),
            pltpu.VMEM((rows, A_ROPE), BF16),
            pltpu.VMEM((rows, SLOT), F32),
            pltpu.VMEM((rows, SLOT), F32),
            pltpu.VMEM((rows, A_KV_LORA), F32),
            pltpu.VMEM((PAGED_SLOTS, ppb * PAGE, A_KV_LORA), F32),
            pltpu.VMEM((PAGED_SLOTS, A_ROPE, ppb * PAGE), F32),
            pltpu.SemaphoreType.DMA((PAGED_SLOTS,)),
        ],
    )
    return pl.pallas_call(
        functools.partial(_paged_kernel, la=la, ppb=ppb, n_blocks=n_blocks),
        grid_spec=grid_spec,
        out_shape=jax.ShapeDtypeStruct((n, 8, A_WIDTH), F32),
        compiler_params=_cparams(("arbitrary", "arbitrary")),
        name="mla_paged",
    )(page_table, q, c_new, kpe_new_t, *params, cache_latent, cache_krope_t)


def _gated_out_kernel(x_ref, o_ref, z_ref, w_ref, y_ref):
    gated = o_ref[...] * _silu(z_ref[...])
    y_ref[...] = x_ref[...] + _dot(gated.astype(BF16), w_ref[...])


def _gated_out(x, o, z, w_o, *, tm):
    r = x.shape[0]
    tm = min(tm, r)
    row = lambda n: pl.BlockSpec((tm, n), lambda i: (i, 0))
    return pl.pallas_call(
        _gated_out_kernel,
        grid=(r // tm,),
        in_specs=[row(D_MODEL), row(o.shape[1]), row(z.shape[1]), pl.BlockSpec(w_o.shape, lambda i: (0, 0))],
        out_specs=row(D_MODEL),
        out_shape=jax.ShapeDtypeStruct((r, D_MODEL), F32),
        compiler_params=_cparams(("parallel",)),
        name="gated_out",
    )(x, o, z, w_o)


def _gdn_proj_kernel(x_ref, gn_ref, w_ref, wabt_ref, qkv_out, z_out, ab_out, abt_out):
    hb = _rms(x_ref[...], gn_ref[...]).astype(BF16)
    proj = _dot(hb, w_ref[...])
    o1 = B_CONV_CH
    o2 = o1 + B_WIDTH
    qkv_out[...] = proj[:, :o1]
    z_out[...] = proj[:, o1:o2]
    ab_out[...] = proj[:, o2:o2 + 2 * B_HEADS]
    abt_out[...] = _dot_nt(wabt_ref[...], hb)


def _gdn_project(x, w, *, tm):
    r = x.shape[0]
    tm = min(tm, r)
    row = lambda n: pl.BlockSpec((tm, n), lambda i: (i, 0))
    full = lambda a: pl.BlockSpec(a.shape, lambda i: (0,) * a.ndim)
    params = [w["b_norm"], w["w_in"], w["w_abt"]]
    return pl.pallas_call(
        _gdn_proj_kernel,
        grid=(r // tm,),
        in_specs=[row(D_MODEL)] + [full(p) for p in params],
        out_specs=[row(B_CONV_CH), row(B_WIDTH), row(2 * B_HEADS), pl.BlockSpec((2 * B_HEADS, tm), lambda i: (0, i))],
        out_shape=[
            jax.ShapeDtypeStruct((r, B_CONV_CH), F32),
            jax.ShapeDtypeStruct((r, B_WIDTH), F32),
            jax.ShapeDtypeStruct((r, 2 * B_HEADS), F32),
            jax.ShapeDtypeStruct((2 * B_HEADS, r), F32),
        ],
        compiler_params=_cparams(("parallel",)),
        name="gdn_proj",
    )(x, *params)


def _gdn_chunk_kernel(qkv_ref, ab_ref, abt_ref, conv0_ref, st0_ref, wc_ref, alr_ref, dtr_ref, alc_ref, dtc_ref,
                      go_ref, exp_ref, tri_ref, triu_ref, off_ref, avg_ref, o_ref, st_out, ext_ref, feat_ref, gcx_ref,
                      bx_ref, st_ref, lm_ref, inv_ref, t1_ref, in_ref, *, t, valid, bb):
    si = pl.program_id(1)
    hk = B_HEADS * B_DK
    units = [(bi, hd) for bi in range(bb) for hd in range(B_HEADS)]
    qsl = lambda hd: slice(hd * B_DK, (hd + 1) * B_DK)
    ksl = lambda hd: slice(hk + hd * B_DK, hk + (hd + 1) * B_DK)
    vsl = lambda hd: slice(2 * hk + hd * B_DV, 2 * hk + (hd + 1) * B_DV)

    @pl.when(si == 0)
    def _():
        ext_ref[:, 0:HALO] = conv0_ref[...]
        st_ref[...] = st0_ref[...]

    gcts = []
    for bi in range(bb):
        ext_ref[bi, HALO:HALO + t] = qkv_ref[bi]
        conv = ext_ref[bi, HALO - 3:HALO - 3 + t] * wc_ref[0:1]
        for j in range(1, B_CONV):
            conv = conv + ext_ref[bi, HALO - 3 + j:HALO - 3 + j + t] * wc_ref[j:j + 1]
        feat_ref[bi] = _silu(conv)
        ext_ref[bi, 0:HALO] = ext_ref[bi, t:t + HALO]

        ab = ab_ref[bi]
        g = -jnp.exp(alr_ref[...]) * _softplus(ab[:, :B_HEADS] + dtr_ref[...])
        beta = jax.nn.sigmoid(ab[:, B_HEADS:])
        if valid < t:
            live = lax.broadcasted_iota(jnp.int32, (t, B_HEADS), 0) < valid
            g = jnp.where(live, g, 0.0)
            beta = jnp.where(live, beta, 0.0)
        bx_ref[bi] = _dot_sel(exp_ref[...], beta, sel_first=False)
        gcx_ref[bi] = _dot_sel(tri_ref[...], _dot_sel(exp_ref[...], g, sel_first=False), sel_first=True)
        gt = -jnp.exp(alc_ref[...]) * _softplus(abt_ref[bi, 0][:B_HEADS] + dtc_ref[...])
        if valid < t:
            gt = jnp.where(lax.broadcasted_iota(jnp.int32, (B_HEADS, t), 1) < valid, gt, 0.0)
        gcts.append(_dot_sel(triu_ref[...], gt, sel_first=False))

    ii = lax.broadcasted_iota(jnp.int32, (t, t), 0)
    jj = lax.broadcasted_iota(jnp.int32, (t, t), 1)
    incl = ii >= jj
    strict = ii > jj
    eye = (ii == jj).astype(F32)
    pair = (ii >> 1) == (jj >> 1)

    def cols(x):
        return x[:, :t] if t <= B_DK else jnp.tile(x, (1, t // B_DK))

    for u, (bi, hd) in enumerate(units):
        qh = feat_ref[bi, :, qsl(hd)]
        kh = feat_ref[bi, :, ksl(hd)]
        qh = qh * lax.rsqrt(jnp.sum(qh * qh, -1, keepdims=True) + EPS) * (B_DK ** -0.5)
        kh = kh * lax.rsqrt(jnp.sum(kh * kh, -1, keepdims=True) + EPS)
        feat_ref[bi, :, qsl(hd)] = qh
        feat_ref[bi, :, ksl(hd)] = kh
        decay = jnp.exp(jnp.where(incl, cols(gcx_ref[bi, :, qsl(hd)]) - gcts[bi][hd:hd + 1, :], -jnp.inf))
        khb = kh.astype(BF16)
        lmat = jnp.where(strict, _dot_nt((kh * bx_ref[bi, :, qsl(hd)]).astype(BF16), khb) * decay, 0.0)
        lm_ref[u] = lmat.astype(BF16)
        in_ref[u] = (_dot_nt(qh.astype(BF16), khb) * decay).astype(BF16)
        inv_ref[u] = (eye - jnp.where(pair, lmat, 0.0)).astype(BF16)

    for lv in range(off_ref.shape[0]):
        for u in range(len(units)):
            t1_ref[u] = _dot(lm_ref[u] * off_ref[lv], inv_ref[u]).astype(BF16)
        for u in range(len(units)):
            inv_ref[u] = inv_ref[u] - _dot(inv_ref[u], t1_ref[u]).astype(BF16)

    go = go_ref[...]
    us, ws, egs = [], [], []
    for u, (bi, hd) in enumerate(units):
        invb = inv_ref[u]
        bc = bx_ref[bi, :, qsl(hd)]
        eg = jnp.exp(gcx_ref[bi, :, qsl(hd)])
        kb = feat_ref[bi, :, ksl(hd)] * bc
        us.append(_dot(invb, (feat_ref[bi, :, vsl(hd)] * bc).astype(BF16)))
        ws.append(_dot(invb, (kb * eg).astype(BF16)).astype(BF16))
        egs.append(eg)
    vns = []
    for u, (bi, hd) in enumerate(units):
        vns.append((us[u] - _dot(ws[u], st_ref[bi, hd].astype(BF16))).astype(BF16))
    for u, (bi, hd) in enumerate(units):
        st = st_ref[bi, hd]
        o = _dot((feat_ref[bi, :, qsl(hd)] * egs[u]).astype(BF16), st.astype(BF16)) + _dot(in_ref[u], vns[u])
        gcc = gcx_ref[bi, :, qsl(hd)]
        glast = gcc[t - 1:t, :]
        kdec = feat_ref[bi, :, ksl(hd)] * jnp.exp(glast - gcc)
        st_ref[bi, hd] = st * jnp.exp(glast) + _dot_tn(kdec.astype(BF16), vns[u])
        if t >= B_DV:
            o2 = o * o
            o2h = o2.astype(BF16)
            ms = _dot(o2h, avg_ref[...]) + _dot((o2 - o2h.astype(F32)).astype(BF16), avg_ref[...])
        else:
            ms = jnp.mean(o * o, -1, keepdims=True)
        o_ref[bi, :, qsl(hd)] = o * lax.rsqrt(ms + EPS) * go

    @pl.when(si == pl.num_programs(1) - 1)
    def _():
        st_out[...] = st_ref[...]


def _gdn_chunked(qkv, ab, abt, conv0, st0, w, *, t, valid, bb):
    b, s, _ = qkv.shape
    t = min(t, s)
    full = lambda a: pl.BlockSpec(a.shape, lambda i, j: (0,) * a.ndim)
    expand = jnp.repeat(jnp.eye(B_HEADS, dtype=BF16), B_DV, axis=1)
    pos = jnp.arange(t)
    tri = (pos[:, None] >= pos[None, :]).astype(BF16)
    triu = (pos[:, None] <= pos[None, :]).astype(BF16)
    blk = lambda lg: pos >> lg
    off = jnp.stack([((blk(lg)[:, None] - blk(lg)[None, :] == 1) & ((blk(lg)[:, None] & 1) == 1)).astype(BF16)
                     for lg in range(1, t.bit_length() - 1)])
    avg = jnp.full((B_DV, B_DV), 1.0 / B_DV, BF16)
    params = [w["w_conv"], w["a_log_r"], w["dt_r"], w["a_log_c"], w["dt_c"], w["g_o"], expand, tri, triu, off, avg]
    bb = min(bb, b)
    assert b % bb == 0 and s % t == 0
    return pl.pallas_call(
        functools.partial(_gdn_chunk_kernel, t=t, valid=valid, bb=bb),
        grid=(b // bb, s // t),
        in_specs=[
            pl.BlockSpec((bb, t, B_CONV_CH), lambda i, j: (i, j, 0)),
            pl.BlockSpec((bb, t, 2 * B_HEADS), lambda i, j: (i, j, 0)),
            pl.BlockSpec((bb, 1, 2 * B_HEADS, t), lambda i, j: (i, j, 0, 0)),
            pl.BlockSpec((bb, HALO, B_CONV_CH), lambda i, j: (i, 0, 0)),
            pl.BlockSpec((bb, B_HEADS, B_DK, B_DV), lambda i, j: (i, 0, 0, 0)),
        ] + [full(p) for p in params],
        out_specs=[
            pl.BlockSpec((bb, t, B_WIDTH), lambda i, j: (i, j, 0)),
            pl.BlockSpec((bb, B_HEADS, B_DK, B_DV), lambda i, j: (i, 0, 0, 0)),
        ],
        out_shape=[
            jax.ShapeDtypeStruct((b, s, B_WIDTH), F32),
            jax.ShapeDtypeStruct((b, B_HEADS, B_DK, B_DV), F32),
        ],
        scratch_shapes=[
            pltpu.VMEM((bb, t + HALO, B_CONV_CH), F32),
            pltpu.VMEM((bb, t, B_CONV_CH), F32),
            pltpu.VMEM((bb, t, B_WIDTH), F32),
            pltpu.VMEM((bb, t, B_WIDTH), F32),
            pltpu.VMEM((bb, B_HEADS, B_DK, B_DV), F32),
            pltpu.VMEM((bb * B_HEADS, t, t), BF16),
            pltpu.VMEM((bb * B_HEADS, t, t), BF16),
            pltpu.VMEM((bb * B_HEADS, t, t), BF16),
            pltpu.VMEM((bb * B_HEADS, t, t), BF16),
        ],
        compiler_params=_cparams(("parallel", "arbitrary")),
        name="gdn_chunk",
    )(qkv, ab, abt, conv0, st0, *params)


def _pad_heads(wm, n_in):
    d = wm.shape[-1]
    return jnp.pad(wm, ((0, 0), (0, 0), (0, SLOT - d))).reshape(n_in, A_HEADS * SLOT)


def _mla_weights(a_norm, a_w_in, a_g_qa, a_w_uq, a_g_kv, a_w_uk, a_w_uv, a_g_q, a_g_k, a_w_o):
    o1 = A_Q_LORA
    o2 = o1 + A_KV_LORA
    o3 = o2 + A_ROPE
    kslot = jnp.pad(a_w_in[:, o2:o3], ((0, 0), (A_NOPE, SLOT - A_QK)))
    w_in = jnp.concatenate([a_w_in[:, :o2], a_w_in[:, o3:], kslot], 1).astype(BF16)
    pad_gain = lambda g: jnp.pad(g, (0, SLOT - A_QK)).reshape(1, SLOT)
    return {
        "a_norm": a_norm.reshape(1, D_MODEL),
        "w_in": w_in,
        "g_qa": a_g_qa.reshape(1, A_Q_LORA),
        "w_uq": _pad_heads(a_w_uq.reshape(A_Q_LORA, A_HEADS, A_QK), A_Q_LORA).astype(BF16),
        "g_kv": a_g_kv.reshape(1, A_KV_LORA),
        "g_q": pad_gain(a_g_q * (A_QK ** -0.5)),
        "g_k": pad_gain(a_g_k),
        "w_uk": _pad_heads(a_w_uk, A_KV_LORA).astype(BF16),
        "w_ukt": jnp.transpose(a_w_uk, (1, 2, 0)).astype(BF16),
        "w_uv": a_w_uv.reshape(A_KV_LORA, A_WIDTH).astype(BF16),
        "w_uv_h": jnp.transpose(a_w_uv, (1, 0, 2)).astype(BF16),
        "w_o": a_w_o.astype(BF16),
    }


def _gdn_weights(b_norm, b_w_in, b_w_conv, b_a_log, b_dt_bias, b_g_o, b_w_o):
    o2 = B_CONV_CH + B_WIDTH
    w_in = jnp.pad(b_w_in, ((0, 0), (0, SLOT - 2 * B_HEADS))).astype(BF16)
    return {
        "b_norm": b_norm.reshape(1, D_MODEL),
        "w_in": w_in,
        "w_abt": b_w_in[:, o2:].T.astype(BF16),
        "w_conv": b_w_conv,
        "a_log_r": b_a_log.reshape(1, B_HEADS),
        "dt_r": b_dt_bias.reshape(1, B_HEADS),
        "a_log_c": b_a_log.reshape(B_HEADS, 1),
        "dt_c": b_dt_bias.reshape(B_HEADS, 1),
        "g_o": b_g_o.reshape(1, B_DV),
        "w_o": b_w_o.astype(BF16),
    }


def _mla_layer(xp, xs, cache_latent, cache_krope, la, page_table, w, tabs_p, tabs_s):
    b, s, _ = xp.shape
    n, t, _ = xs.shape
    q, c_p, kpe_p, z, k, v = _mla_project(xp, tabs_p, w, with_kv=True, tm=256, q_dtype=BF16)
    bound = A_QK * BF16_NORM_SLACK * jnp.max(jnp.abs(w["g_q"])) * jnp.max(jnp.abs(w["g_k"]))
    o = _flash_attention(q, k, v, bound, tq=1024, tk=512)
    yp = _gated_out(xp.reshape(b * s, D_MODEL), o.reshape(b * s, A_WIDTH), z.reshape(b * s, A_WIDTH), w["w_o"], tm=512)

    q, c_s, kpe_s, z = _mla_project(xs.reshape(1, n * t, D_MODEL), tabs_s, w, with_kv=False, tm=256, q_dtype=F32)
    c_s = c_s.reshape(n, t, A_KV_LORA)
    kpe_s = kpe_s.reshape(n, t, A_ROPE)
    q8 = jnp.pad(q.reshape(n, t, A_QPAD), ((0, 0), (0, 8 - t), (0, 0)))
    c_new = jnp.pad(c_s, ((0, 0), (0, PAGE - t), (0, 0)))
    kpe_new_t = jnp.pad(jnp.swapaxes(kpe_s, 1, 2), ((0, 0), (0, 0), (0, PAGE - t)))
    o = _paged_attention(q8, c_new, kpe_new_t, cache_latent, jnp.swapaxes(cache_krope, 2, 3), la, page_table, w,
                         ppb=16)
    ys = _gated_out(xs.reshape(n * t, D_MODEL), o[:, :t].reshape(n * t, A_WIDTH), z.reshape(n * t, A_WIDTH),
                    w["w_o"], tm=512)
    return yp.reshape(b, s, D_MODEL), ys.reshape(n, t, D_MODEL), c_p, kpe_p, c_s, kpe_s


def _gdn_layer(xp, xs, state_conv, state_ssm, w):
    b, s, _ = xp.shape
    n, t, _ = xs.shape
    tc = min(256, s)
    qkv, z, ab, abt = _gdn_project(xp.reshape(b * s, D_MODEL), w, tm=256)
    qkv = qkv.reshape(b, s, B_CONV_CH)
    abt = abt.reshape(2 * B_HEADS, b, s // tc, tc).transpose(1, 2, 0, 3)
    o, st_p = _gdn_chunked(qkv, ab.reshape(b, s, 2 * B_HEADS), abt,
                           jnp.zeros((b, HALO, B_CONV_CH), F32), jnp.zeros((b, B_HEADS, B_DK, B_DV), F32),
                           w, t=tc, valid=tc, bb=1)
    yp = _gated_out(xp.reshape(b * s, D_MODEL), o.reshape(b * s, B_WIDTH), z, w["w_o"], tm=512)
    conv_p = qkv[:, s - (B_CONV - 1):]

    qkv, z, ab, abt = _gdn_project(xs.reshape(n * t, D_MODEL), w, tm=256)
    qkv = qkv.reshape(n, t, B_CONV_CH)
    pad_t = lambda a: jnp.pad(a, ((0, 0), (0, 8 - t), (0, 0)))
    abt = jnp.pad(abt.reshape(2 * B_HEADS, n, t).transpose(1, 0, 2), ((0, 0), (0, 0), (0, 8 - t)))
    conv0 = jnp.pad(state_conv, ((0, 0), (HALO - (B_CONV - 1), 0), (0, 0)))
    o, st_s = _gdn_chunked(pad_t(qkv), pad_t(ab.reshape(n, t, 2 * B_HEADS)), abt.reshape(n, 1, 2 * B_HEADS, 8),
                           conv0, state_ssm, w, t=8, valid=t, bb=4)
    ys = _gated_out(xs.reshape(n * t, D_MODEL), o[:, :t].reshape(n * t, B_WIDTH), z, w["w_o"], tm=512)
    conv_s = jnp.concatenate([state_conv, qkv], 1)[:, -(B_CONV - 1):]
    return yp.reshape(b, s, D_MODEL), ys.reshape(n, t, D_MODEL), conv_p, st_p, conv_s, st_s


def kernel(x_prompt, x_sample, cache_latent, cache_krope, page_table, state_conv, state_ssm,
           a_norm, a_w_in, a_g_qa, a_w_uq, a_g_kv, a_w_uk, a_w_uv, a_g_q, a_g_k, a_w_o,
           b_norm, b_w_in, b_w_conv, b_a_log, b_dt_bias, b_g_o, b_w_o):
    s = x_prompt.shape[1]
    n, t, _ = x_sample.shape
    past = page_table.shape[1] * PAGE
    p_pad = -(-(s + t) // SLOT) * SLOT
    pos = jnp.concatenate([jnp.arange(s), past + jnp.arange(t), jnp.zeros((p_pad - s - t,), jnp.int32)]).astype(F32)
    tabs = _rope_tables(pos)
    tabs_p = tuple(tb[:s] for tb in tabs)
    tabs_s = tuple(jnp.tile(tb[s:s + t], (n, 1)) for tb in tabs)

    wa = _mla_weights(a_norm[0], a_w_in[0], a_g_qa[0], a_w_uq[0], a_g_kv[0], a_w_uk[0], a_w_uv[0], a_g_q[0],
                      a_g_k[0], a_w_o[0])
    wb = _gdn_weights(b_norm[0], b_w_in[0], b_w_conv[0], b_a_log[0], b_dt_bias[0], b_g_o[0], b_w_o[0])

    xp, xs, lat_p, kpe_p, lat_s, kpe_s = _mla_layer(x_prompt, x_sample, cache_latent, cache_krope, 0, page_table,
                                                    wa, tabs_p, tabs_s)
    xp, xs, conv_p, ssm_p, conv_s, ssm_s = _gdn_layer(xp, xs, state_conv[0], state_ssm[0], wb)
    return (xp, xs, lat_p[None], kpe_p[None], lat_s[None], kpe_s[None],
            conv_p[None], ssm_p[None], conv_s[None], ssm_s[None])
```

```python
import functools

import jax
import jax.numpy as jnp
from jax import lax
from jax.experimental import pallas as pl
from jax.experimental.pallas import tpu as pltpu

F32 = jnp.float32
BF16 = jnp.bfloat16
EPS = 1e-6

D_MODEL = 1024
PAGE = 128
A_HEADS = 8
A_NOPE = 64
A_ROPE = 32
A_QK = A_NOPE + A_ROPE
A_V = 64
A_Q_LORA = 384
A_KV_LORA = 256
A_WIDTH = A_HEADS * A_V
ROPE_THETA = 10000.0
SLOT = 128
A_QPAD = A_HEADS * SLOT
B_HEADS = 8
B_DK = 64
B_DV = 64
B_WIDTH = B_HEADS * B_DV
B_CONV = 4
B_CONV_CH = 2 * B_HEADS * B_DK + B_WIDTH
HALO = 8
BF16_ROWS = 16

VMEM_LIMIT = 56 * 1024 * 1024
FIXED_SHIFT_MAX = 30.0
BF16_NORM_SLACK = 1.02
PAGED_SLOTS = 3
PAGED_PARTS = 2


def _cparams(sem):
    return pltpu.CompilerParams(dimension_semantics=sem, vmem_limit_bytes=VMEM_LIMIT)


def _dot(a, b):
    return jnp.dot(a, b, preferred_element_type=F32)


def _dot_nt(a, b):
    return lax.dot_general(a, b, (((1,), (1,)), ((), ())), preferred_element_type=F32)


def _dot_tn(a, b):
    return lax.dot_general(a, b, (((0,), (0,)), ((), ())), preferred_element_type=F32)


def _dot_sel(sel_bf16, x, *, sel_first):
    x1 = x.astype(BF16)
    r1 = x - x1.astype(F32)
    x2 = r1.astype(BF16)
    x3 = (r1 - x2.astype(F32)).astype(BF16)
    if sel_first:
        return _dot(sel_bf16, x1) + _dot(sel_bf16, x2) + _dot(sel_bf16, x3)
    return _dot(x1, sel_bf16) + _dot(x2, sel_bf16) + _dot(x3, sel_bf16)


def _rms(x, g):
    return x * lax.rsqrt(jnp.mean(x * x, -1, keepdims=True) + EPS) * g


def _silu(x):
    return x * jax.nn.sigmoid(x)


def _softplus(x):
    return jnp.maximum(x, 0.0) + jnp.log(1.0 + jnp.exp(-jnp.abs(x)))


def _rope_table_kernel(pos_ref, inv_ref, cos_ref, sin_ref):
    ang = inv_ref[...] * pos_ref[...]
    cos_ref[...] = jnp.cos(ang)
    sin_ref[...] = jnp.sin(ang)


def _rope_tables(pos):
    half = A_ROPE // 2
    p = pos.shape[0]
    inv = (ROPE_THETA ** (-jnp.arange(half, dtype=F32) / half)).reshape(half, 1)
    cos_t, sin_t = pl.pallas_call(
        _rope_table_kernel,
        out_shape=(jax.ShapeDtypeStruct((half, p), F32),) * 2,
        name="rope_tables",
    )(pos.reshape(1, p), inv)
    cos = cos_t.T
    sin = sin_t.T
    one = jnp.ones((p, A_NOPE), F32)
    zn = jnp.zeros((p, A_NOPE), F32)
    zh = jnp.zeros((p, half), F32)
    zp = jnp.zeros((p, SLOT - A_QK), F32)
    tc = jnp.concatenate([one, cos, cos, zp], 1)
    ts1 = jnp.concatenate([zn, -sin, zh, zp], 1)
    ts2 = jnp.concatenate([zn, zh, sin, zp], 1)
    return tc, ts1, ts2


def _mla_proj_kernel(*refs, with_kv):
    (x_ref, tc_ref, ts1_ref, ts2_ref, gn_ref, win_ref, gqa_ref, wuq_ref, gkv_ref, gq_ref, ones_ref) = refs[:11]
    if with_kv:
        wuk_ref, gk_ref, wuv_ref = refs[11:14]
        q_out, c_out, kpe_out, z_out, k_out, v_out = refs[14:]
    else:
        q_out, c_out, kpe_out, z_out = refs[11:]
    x = x_ref[0]
    h = _rms(x, gn_ref[...])
    proj = _dot(h.astype(BF16), win_ref[...])
    o1 = A_Q_LORA
    o2 = o1 + A_KV_LORA
    o3 = o2 + A_WIDTH
    qa = _rms(proj[:, :o1], gqa_ref[...])
    c = _rms(proj[:, o1:o2], gkv_ref[...])
    z_out[0] = proj[:, o2:o3]
    c_out[0] = c
    q = _dot(qa.astype(BF16), wuq_ref[...])
    tc = tc_ref[...]
    ts1 = ts1_ref[...]
    ts2 = ts2_ref[...]

    def rope(s):
        return s * tc + pltpu.roll(s, SLOT - A_ROPE // 2, 1) * ts1 + pltpu.roll(s, A_ROPE // 2, 1) * ts2

    ones = ones_ref[...]

    def head_norm(s, g):
        s2 = s * s
        s2h = s2.astype(BF16)
        ms = (_dot(s2h, ones) + _dot((s2 - s2h.astype(F32)).astype(BF16), ones)) * (1.0 / A_QK)
        return s * lax.rsqrt(ms + EPS) * g

    kslot = rope(proj[:, o3:o3 + SLOT])
    kpe_out[0] = kslot[:, A_NOPE:A_QK]
    gq = gq_ref[...]
    for hd in range(A_HEADS):
        sl = slice(hd * SLOT, (hd + 1) * SLOT)
        q_out[0, :, sl] = head_norm(rope(q[:, sl]), gq).astype(q_out.dtype)
    if with_kv:
        cb = c.astype(BF16)
        kn = _dot(cb, wuk_ref[...])
        v_out[0] = _dot(cb, wuv_ref[...]).astype(v_out.dtype)
        gk = gk_ref[...]
        for hd in range(A_HEADS):
            sl = slice(hd * SLOT, (hd + 1) * SLOT)
            k_out[0, :, sl] = head_norm(kn[:, sl] + kslot, gk).astype(k_out.dtype)


def _mla_project(x, tabs, w, *, with_kv, tm, q_dtype):
    b, s, _ = x.shape
    tm = min(tm, s)
    grid = (s // tm, b)
    full = lambda a: pl.BlockSpec(a.shape, lambda i, j: (0,) * a.ndim)
    row = lambda n: pl.BlockSpec((1, tm, n), lambda i, j: (j, i, 0))
    tab = pl.BlockSpec((tm, SLOT), lambda i, j: (i, 0))
    params = [w["a_norm"], w["w_in"], w["g_qa"], w["w_uq"], w["g_kv"], w["g_q"], jnp.ones((SLOT, SLOT), BF16)]
    out_shape = [
        jax.ShapeDtypeStruct((b, s, A_QPAD), q_dtype),
        jax.ShapeDtypeStruct((b, s, A_KV_LORA), F32),
        jax.ShapeDtypeStruct((b, s, A_ROPE), F32),
        jax.ShapeDtypeStruct((b, s, A_WIDTH), F32),
    ]
    out_specs = [row(A_QPAD), row(A_KV_LORA), row(A_ROPE), row(A_WIDTH)]
    if with_kv:
        params += [w["w_uk"], w["g_k"], w["w_uv"]]
        out_shape += [jax.ShapeDtypeStruct((b, s, A_QPAD), BF16), jax.ShapeDtypeStruct((b, s, A_WIDTH), BF16)]
        out_specs += [row(A_QPAD), row(A_WIDTH)]
    return pl.pallas_call(
        functools.partial(_mla_proj_kernel, with_kv=with_kv),
        grid=grid,
        in_specs=[row(D_MODEL), tab, tab, tab] + [full(p) for p in params],
        out_specs=out_specs,
        out_shape=out_shape,
        compiler_params=_cparams(("parallel", "parallel")),
        name="mla_proj_kv" if with_kv else "mla_proj",
    )(x, *tabs, *params)


def _flash_kernel(qi_ref, ki_ref, dv_ref, fx_ref, q_ref, k_ref, v_ref, sh_ref, o_ref, m_ref, l_ref, acc_ref,
                  *, tq, tk):
    del qi_ref
    p = pl.program_id(1)
    ki = ki_ref[p]
    dv = dv_ref[p]
    fixed = fx_ref[0] == 1

    @pl.when(ki == 0)
    def _():
        m_ref[...] = jnp.full(m_ref.shape, -jnp.inf, F32)
        l_ref[...] = jnp.zeros(l_ref.shape, F32)
        acc_ref[...] = jnp.zeros(acc_ref.shape, F32)

    low = lax.broadcasted_iota(jnp.int32, (tq, SLOT), 1) < A_V

    def step(d, fixed_shift):
        diag = d is not None
        if diag:
            keep = (lax.broadcasted_iota(jnp.int32, (tq, tk), 1) + d * tk
                    <= lax.broadcasted_iota(jnp.int32, (tq, tk), 0))
        if fixed_shift:
            shift = sh_ref[:, :1]
        for j in range(A_HEADS // 2):
            pv = []
            al = []
            vpair = v_ref[0, :, j * SLOT:(j + 1) * SLOT]
            for e in range(2):
                hd = 2 * j + e
                sl = slice(hd * SLOT, (hd + 1) * SLOT)
                s = _dot_nt(q_ref[0, :, sl], k_ref[0, :, sl])
                if diag:
                    s = jnp.where(keep, s, -jnp.inf)
                if fixed_shift:
                    pr = jnp.exp(s - shift)
                    l_ref[hd] = l_ref[hd] + jnp.sum(pr, -1, keepdims=True)
                else:
                    m_prev = m_ref[hd]
                    m_new = jnp.maximum(m_prev, jnp.max(s, -1, keepdims=True))
                    alpha = jnp.exp(m_prev - m_new)
                    pr = jnp.exp(s - m_new[:, :1])
                    l_ref[hd] = alpha * l_ref[hd] + jnp.sum(pr, -1, keepdims=True)
                    m_ref[hd] = m_new
                    al.append(alpha)
                pv.append(_dot(pr.astype(BF16), vpair))
            sl = slice(j * SLOT, (j + 1) * SLOT)
            if fixed_shift:
                acc_ref[:, sl] = acc_ref[:, sl] + jnp.where(low, pv[0], pv[1])
            else:
                acc_ref[:, sl] = jnp.where(low, al[0], al[1]) * acc_ref[:, sl] + jnp.where(low, pv[0], pv[1])

    def finish():
        for j in range(A_HEADS // 2):
            sl = slice(j * SLOT, (j + 1) * SLOT)
            linv = jnp.where(low, 1.0 / l_ref[2 * j], 1.0 / l_ref[2 * j + 1])
            o_ref[0, :, sl] = acc_ref[:, sl] * linv

    n_diag = tq // tk
    for fs in (True, False):
        mode = fixed if fs else jnp.logical_not(fixed)

        @pl.when(jnp.logical_and(mode, dv < 0))
        def _():
            step(None, fs)

        for d in range(n_diag):
            @pl.when(jnp.logical_and(mode, dv == d))
            def _():
                step(d, fs)
                if d == n_diag - 1:
                    finish()


def _flash_attention(q, k, v, bound, *, tq, tk):
    b, s, _ = q.shape
    tq = min(tq, s)
    tk = min(tk, tq)
    assert s % tq == 0 and tq % tk == 0
    nq = s // tq
    r = tq // tk
    pairs = [(i, j, j - r * i if j >= r * i else -1) for i in range(nq) for j in range(r * (i + 1))]
    qi = jnp.asarray([pr[0] for pr in pairs], jnp.int32)
    ki = jnp.asarray([pr[1] for pr in pairs], jnp.int32)
    dv = jnp.asarray([pr[2] for pr in pairs], jnp.int32)
    fx = (bound <= FIXED_SHIFT_MAX).astype(jnp.int32).reshape(1)
    sh = jnp.full((1, SLOT), bound, F32)
    grid_spec = pltpu.PrefetchScalarGridSpec(
        num_scalar_prefetch=4,
        grid=(b, len(pairs)),
        in_specs=[
            pl.BlockSpec((1, tq, A_QPAD), lambda bi, p, qt, kt, dt, fx: (bi, qt[p], 0)),
            pl.BlockSpec((1, tk, A_QPAD), lambda bi, p, qt, kt, dt, fx: (bi, kt[p], 0)),
            pl.BlockSpec((1, tk, A_WIDTH), lambda bi, p, qt, kt, dt, fx: (bi, kt[p], 0)),
            pl.BlockSpec((1, SLOT), lambda bi, p, qt, kt, dt, fx: (0, 0)),
        ],
        out_specs=pl.BlockSpec((1, tq, A_WIDTH), lambda bi, p, qt, kt, dt, fx: (bi, qt[p], 0)),
        scratch_shapes=[
            pltpu.VMEM((A_HEADS, tq, SLOT), F32),
            pltpu.VMEM((A_HEADS, tq, SLOT), F32),
            pltpu.VMEM((tq, A_WIDTH), F32),
        ],
    )
    return pl.pallas_call(
        functools.partial(_flash_kernel, tq=tq, tk=tk),
        grid_spec=grid_spec,
        out_shape=jax.ShapeDtypeStruct((b, s, A_WIDTH), F32),
        compiler_params=_cparams(("parallel", "arbitrary")),
        name="mla_flash",
    )(qi, ki, dv, fx, q, k, v, sh)


def _paged_kernel(pt_ref, q_ref, cn_ref, kn_ref, gk_ref, wukt_ref, wuv_ref, lat_hbm, kr_hbm, o_ref,
                  wq_ref, qr_ref, m_ref, l_ref, acc_ref, lat_buf, kr_buf, sem, *, la, ppb, n_blocks):
    smp = pl.program_id(0)
    blk = pl.program_id(1)
    rows = A_HEADS * 8
    hn = A_HEADS * A_NOPE

    step = smp * n_blocks + blk
    n_steps = pl.num_programs(0) * n_blocks
    ahead = PAGED_SLOTS - 1

    def page_copies(g):
        s_ = g // n_blocks
        b_ = g % n_blocks
        slot_ = g % PAGED_SLOTS
        out = []
        for i in range(ppb):
            page = pt_ref[s_, b_ * ppb + i]
            out.append(pltpu.make_async_copy(lat_hbm.at[la, page], lat_buf.at[slot_, pl.ds(i * PAGE, PAGE)],
                                             sem.at[slot_]))
            out.append(pltpu.make_async_copy(kr_hbm.at[la, page], kr_buf.at[slot_, :, pl.ds(i * PAGE, PAGE)],
                                             sem.at[slot_]))
        return out

    for d in range(ahead):
        @pl.when(jnp.logical_and(step == 0, d < n_steps))
        def _():
            for cp in page_copies(step + d):
                cp.start()

    @pl.when(step + ahead < n_steps)
    def _():
        for cp in page_copies(step + ahead):
            cp.start()

    for cp in page_copies(step):
        cp.wait()
    slot = step % PAGED_SLOTS

    @pl.when(blk == 0)
    def _():
        m_ref[...] = jnp.full(m_ref.shape, -jnp.inf, F32)
        l_ref[...] = jnp.zeros(l_ref.shape, F32)
        acc_ref[...] = jnp.zeros(acc_ref.shape, F32)
        gk = gk_ref[...]
        for hd in range(A_HEADS):
            qs = q_ref[0, :, hd * SLOT:(hd + 1) * SLOT] * gk
            wq_ref[hd * A_NOPE:(hd + 1) * A_NOPE, :] = wukt_ref[hd]
            wq_ref[hn + hd * 8:hn + (hd + 1) * 8, :] = _dot(qs[:, :A_NOPE].astype(BF16), wukt_ref[hd]).astype(BF16)
            qr_ref[hd * 8:(hd + 1) * 8, :] = qs[:, A_NOPE:A_QK].astype(BF16)

    def scores(cb, krt):
        nk = cb.shape[0]
        res = _dot_nt(wq_ref[...], cb)
        ssq = []
        for hd in range(A_HEADS):
            kh = res[hd * A_NOPE:(hd + 1) * A_NOPE]
            ssq.append(jnp.broadcast_to(jnp.sum(kh * kh, 0, keepdims=True), (8, nk)))
        ssq = jnp.concatenate(ssq, 0) + jnp.sum(krt * krt, 0, keepdims=True)
        s = res[hn:] + _dot(qr_ref[...], krt.astype(BF16))
        return s * lax.rsqrt(ssq * (1.0 / A_QK) + EPS)

    def block(cbs, krts, new):
        sts = [scores(cb, krt) for cb, krt in zip(cbs, krts)]
        if new:
            nk = cbs[0].shape[0]
            tok = lax.broadcasted_iota(jnp.int32, (rows, nk), 0) & 7
            key = lax.broadcasted_iota(jnp.int32, (rows, nk), 1)
            sts = [jnp.where(key <= tok, st, -jnp.inf) for st in sts]
        ms = [jnp.max(st, -1, keepdims=True) for st in sts]
        prs = [jnp.exp(st - m) for st, m in zip(sts, ms)]
        ls = [jnp.sum(pr, -1, keepdims=True) for pr in prs]
        pvs = [_dot(pr.astype(BF16), cb) for pr, cb in zip(prs, cbs)]
        m_prev = m_ref[...]
        m_new = m_prev
        for m in ms:
            m_new = jnp.maximum(m_new, m)
        alpha = jnp.exp(m_prev - m_new)
        l_new = alpha * l_ref[...]
        acc = jnp.concatenate([alpha, alpha], 1) * acc_ref[...]
        for m, l, pv in zip(ms, ls, pvs):
            wgt = jnp.exp(m - m_new)
            l_new = l_new + wgt * l
            acc = acc + jnp.concatenate([wgt, wgt], 1) * pv
        m_ref[...] = m_new
        l_ref[...] = l_new
        acc_ref[...] = acc

    sub = max(ppb // PAGED_PARTS, 1) * PAGE
    block([lat_buf[slot, i:i + sub].astype(BF16) for i in range(0, ppb * PAGE, sub)],
          [kr_buf[slot, :, i:i + sub] for i in range(0, ppb * PAGE, sub)], False)

    @pl.when(blk == n_blocks - 1)
    def _():
        block([cn_ref[0].astype(BF16)], [kn_ref[0]], True)
        o_lat = (acc_ref[...] * (1.0 / l_ref[...][:, :1])).astype(BF16)
        for hd in range(A_HEADS):
            o_ref[0, :, hd * A_V:(hd + 1) * A_V] = _dot(o_lat[hd * 8:(hd + 1) * 8], wuv_ref[hd])


def _paged_attention(q, c_new, kpe_new_t, cache_latent, cache_krope_t, la, page_table, w, *, ppb):
    n = q.shape[0]
    n_pages = page_table.shape[1]
    ppb = min(ppb, n_pages)
    n_blocks = n_pages // ppb
    full = lambda a: pl.BlockSpec(a.shape, lambda i, j, pt: (0,) * a.ndim)
    rows = A_HEADS * 8
    params = [w["g_k"], w["w_ukt"], w["w_uv_h"]]

    hbm = pl.BlockSpec(memory_space=pl.ANY)
    grid_spec = pltpu.PrefetchScalarGridSpec(
        num_scalar_prefetch=1,
        grid=(n, n_blocks),
        in_specs=[
            pl.BlockSpec((1, 8, A_QPAD), lambda s, j, pt: (s, 0, 0)),
            pl.BlockSpec((1, PAGE, A_KV_LORA), lambda s, j, pt: (s, 0, 0)),
            pl.BlockSpec((1, A_ROPE, PAGE), lambda s, j, pt: (s, 0, 0)),
        ] + [full(p) for p in params] + [hbm, hbm],
        out_specs=pl.BlockSpec((1, 8, A_WIDTH), lambda s, j, pt: (s, 0, 0)),
        scratch_shapes=[
            pltpu.VMEM((A_HEADS * A_NOPE + rows, A_KV_LORA), BF16),
            pltpu.VMEM((rows, A_ROPE), BF16),
            pltpu.VMEM((rows, SLOT), F32),
            pltpu.VMEM((rows, SLOT), F32),
            pltpu.VMEM((rows, A_KV_LORA), F32),
            pltpu.VMEM((PAGED_SLOTS, ppb * PAGE, A_KV_LORA), F32),
            pltpu.VMEM((PAGED_SLOTS, A_ROPE, ppb * PAGE), F32),
            pltpu.SemaphoreType.DMA((PAGED_SLOTS,)),
        ],
    )
    return pl.pallas_call(
        functools.partial(_paged_kernel, la=la, ppb=ppb, n_blocks=n_blocks),
        grid_spec=grid_spec,
        out_shape=jax.ShapeDtypeStruct((n, 8, A_WIDTH), F32),
        compiler_params=_cparams(("arbitrary", "arbitrary")),
        name="mla_paged",
    )(page_table, q, c_new, kpe_new_t, *params, cache_latent, cache_krope_t)


def _gated_out_kernel(x_ref, o_ref, z_ref, w_ref, y_ref):
    gated = o_ref[...] * _silu(z_ref[...])
    y_ref[...] = x_ref[...] + _dot(gated.astype(BF16), w_ref[...])


def _gated_out(x, o, z, w_o, *, tm):
    r = x.shape[0]
    tm = min(tm, r)
    row = lambda n: pl.BlockSpec((tm, n), lambda i: (i, 0))
    return pl.pallas_call(
        _gated_out_kernel,
        grid=(r // tm,),
        in_specs=[row(D_MODEL), row(o.shape[1]), row(z.shape[1]), pl.BlockSpec(w_o.shape, lambda i: (0, 0))],
        out_specs=row(D_MODEL),
        out_shape=jax.ShapeDtypeStruct((r, D_MODEL), F32),
        compiler_params=_cparams(("parallel",)),
        name="gated_out",
    )(x, o, z, w_o)


def _gdn_proj_kernel(x_ref, gn_ref, w_ref, wabt_ref, qkv_out, z_out, ab_out, abt_out):
    hb = _rms(x_ref[...], gn_ref[...]).astype(BF16)
    proj = _dot(hb, w_ref[...])
    o1 = B_CONV_CH
    o2 = o1 + B_WIDTH
    qkv_out[...] = proj[:, :o1]
    z_out[...] = proj[:, o1:o2]
    ab_out[...] = proj[:, o2:o2 + 2 * B_HEADS]
    abt_out[...] = _dot_nt(wabt_ref[...], hb)


def _gdn_project(x, w, *, tm):
    r = x.shape[0]
    tm = min(tm, r)
    row = lambda n: pl.BlockSpec((tm, n), lambda i: (i, 0))
    full = lambda a: pl.BlockSpec(a.shape, lambda i: (0,) * a.ndim)
    params = [w["b_norm"], w["w_in"], w["w_abt"]]
    return pl.pallas_call(
        _gdn_proj_kernel,
        grid=(r // tm,),
        in_specs=[row(D_MODEL)] + [full(p) for p in params],
        out_specs=[row(B_CONV_CH), row(B_WIDTH), row(2 * B_HEADS), pl.BlockSpec((2 * B_HEADS, tm), lambda i: (0, i))],
        out_shape=[
            jax.ShapeDtypeStruct((r, B_CONV_CH), F32),
            jax.ShapeDtypeStruct((r, B_WIDTH), F32),
            jax.ShapeDtypeStruct((r, 2 * B_HEADS), F32),
            jax.ShapeDtypeStruct((2 * B_HEADS, r), F32),
        ],
        compiler_params=_cparams(("parallel",)),
        name="gdn_proj",
    )(x, *params)


def _gdn_chunk_kernel(qkv_ref, ab_ref, abt_ref, conv0_ref, st0_ref, wc_ref, alr_ref, dtr_ref, alc_ref, dtc_ref,
                      go_ref, exp_ref, tri_ref, triu_ref, off_ref, avg_ref, o_ref, st_out, ext_ref, feat_ref, gcx_ref,
                      bx_ref, st_ref, lm_ref, inv_ref, t1_ref, in_ref, *, t, valid, bb):
    si = pl.program_id(1)
    hk = B_HEADS * B_DK
    units = [(bi, hd) for bi in range(bb) for hd in range(B_HEADS)]
    qsl = lambda hd: slice(hd * B_DK, (hd + 1) * B_DK)
    ksl = lambda hd: slice(hk + hd * B_DK, hk + (hd + 1) * B_DK)
    vsl = lambda hd: slice(2 * hk + hd * B_DV, 2 * hk + (hd + 1) * B_DV)

    @pl.when(si == 0)
    def _():
        ext_ref[:, 0:HALO] = conv0_ref[...]
        st_ref[...] = st0_ref[...]

    gcts = []
    for bi in range(bb):
        ext_ref[bi, HALO:HALO + t] = qkv_ref[bi]
        conv = ext_ref[bi, HALO - 3:HALO - 3 + t] * wc_ref[0:1]
        for j in range(1, B_CONV):
            conv = conv + ext_ref[bi, HALO - 3 + j:HALO - 3 + j + t] * wc_ref[j:j + 1]
        feat_ref[bi] = _silu(conv)
        ext_ref[bi, 0:HALO] = ext_ref[bi, t:t + HALO]

        ab = ab_ref[bi]
        g = -jnp.exp(alr_ref[...]) * _softplus(ab[:, :B_HEADS] + dtr_ref[...])
        beta = jax.nn.sigmoid(ab[:, B_HEADS:])
        if valid < t:
            live = lax.broadcasted_iota(jnp.int32, (t, B_HEADS), 0) < valid
            g = jnp.where(live, g, 0.0)
            beta = jnp.where(live, beta, 0.0)
        bx_ref[bi] = _dot_sel(exp_ref[...], beta, sel_first=False)
        gcx_ref[bi] = _dot_sel(tri_ref[...], _dot_sel(exp_ref[...], g, sel_first=False), sel_first=True)
        gt = -jnp.exp(alc_ref[...]) * _softplus(abt_ref[bi, 0][:B_HEADS] + dtc_ref[...])
        if valid < t:
            gt = jnp.where(lax.broadcasted_iota(jnp.int32, (B_HEADS, t), 1) < valid, gt, 0.0)
        gcts.append(_dot_sel(triu_ref[...], gt, sel_first=False))

    ii = lax.broadcasted_iota(jnp.int32, (t, t), 0)
    jj = lax.broadcasted_iota(jnp.int32, (t, t), 1)
    incl = ii >= jj
    strict = ii > jj
    eye = (ii == jj).astype(F32)
    pair = (ii >> 1) == (jj >> 1)

    def cols(x):
        return x[:, :t] if t <= B_DK else jnp.tile(x, (1, t // B_DK))

    for u, (bi, hd) in enumerate(units):
        qh = feat_ref[bi, :, qsl(hd)]
        kh = feat_ref[bi, :, ksl(hd)]
        qh = qh * lax.rsqrt(jnp.sum(qh * qh, -1, keepdims=True) + EPS) * (B_DK ** -0.5)
        kh = kh * lax.rsqrt(jnp.sum(kh * kh, -1, keepdims=True) + EPS)
        feat_ref[bi, :, qsl(hd)] = qh
        feat_ref[bi, :, ksl(hd)] = kh
        decay = jnp.exp(jnp.where(incl, cols(gcx_ref[bi, :, qsl(hd)]) - gcts[bi][hd:hd + 1, :], -jnp.inf))
        khb = kh.astype(BF16)
        lmat = jnp.where(strict, _dot_nt((kh * bx_ref[bi, :, qsl(hd)]).astype(BF16), khb) * decay, 0.0)
        lm_ref[u] = lmat.astype(BF16)
        in_ref[u] = (_dot_nt(qh.astype(BF16), khb) * decay).astype(BF16)
        inv_ref[u] = (eye - jnp.where(pair, lmat, 0.0)).astype(BF16)

    for lv in range(off_ref.shape[0]):
        bsz = 2 << lv
        if bsz % BF16_ROWS == 0:
            rngs = [((2 * m + 1) * bsz, (2 * m + 2) * bsz) for m in range(t // (2 * bsz))]
            rows_of = lambda ref, i: jnp.concatenate([ref[i, a:b, :] for a, b in rngs], 0)
            zero = jnp.zeros((bsz, t), BF16)
            for u in range(len(units)):
                half = _dot(rows_of(lm_ref, u) * rows_of(off_ref, lv), inv_ref[u]).astype(BF16)
                pieces = []
                for m in range(len(rngs)):
                    pieces += [zero, half[m * bsz:(m + 1) * bsz]]
                t1_ref[u] = jnp.concatenate(pieces, 0)
            for u in range(len(units)):
                upd = _dot(rows_of(inv_ref, u), t1_ref[u]).astype(BF16)
                for m, (a, b) in enumerate(rngs):
                    inv_ref[u, a:b, :] = inv_ref[u, a:b, :] - upd[m * bsz:(m + 1) * bsz]
        else:
            for u in range(len(units)):
                t1_ref[u] = _dot(lm_ref[u] * off_ref[lv], inv_ref[u]).astype(BF16)
            for u in range(len(units)):
                inv_ref[u] = inv_ref[u] - _dot(inv_ref[u], t1_ref[u]).astype(BF16)

    go = go_ref[...]
    us, ws, egs = [], [], []
    for u, (bi, hd) in enumerate(units):
        invb = inv_ref[u]
        bc = bx_ref[bi, :, qsl(hd)]
        eg = jnp.exp(gcx_ref[bi, :, qsl(hd)])
        kb = feat_ref[bi, :, ksl(hd)] * bc
        us.append(_dot(invb, (feat_ref[bi, :, vsl(hd)] * bc).astype(BF16)))
        ws.append(_dot(invb, (kb * eg).astype(BF16)).astype(BF16))
        egs.append(eg)
    vns = []
    for u, (bi, hd) in enumerate(units):
        vns.append((us[u] - _dot(ws[u], st_ref[bi, hd].astype(BF16))).astype(BF16))
    for u, (bi, hd) in enumerate(units):
        st = st_ref[bi, hd]
        o = _dot((feat_ref[bi, :, qsl(hd)] * egs[u]).astype(BF16), st.astype(BF16)) + _dot(in_ref[u], vns[u])
        gcc = gcx_ref[bi, :, qsl(hd)]
        glast = gcc[t - 1:t, :]
        kdec = feat_ref[bi, :, ksl(hd)] * jnp.exp(glast - gcc)
        st_ref[bi, hd] = st * jnp.exp(glast) + _dot_tn(kdec.astype(BF16), vns[u])
        if t >= B_DV:
            o2 = o * o
            o2h = o2.astype(BF16)
            ms = _dot(o2h, avg_ref[...]) + _dot((o2 - o2h.astype(F32)).astype(BF16), avg_ref[...])
        else:
            ms = jnp.mean(o * o, -1, keepdims=True)
        o_ref[bi, :, qsl(hd)] = o * lax.rsqrt(ms + EPS) * go

    @pl.when(si == pl.num_programs(1) - 1)
    def _():
        st_out[...] = st_ref[...]


def _gdn_chunked(qkv, ab, abt, conv0, st0, w, *, t, valid, bb):
    b, s, _ = qkv.shape
    t = min(t, s)
    full = lambda a: pl.BlockSpec(a.shape, lambda i, j: (0,) * a.ndim)
    expand = jnp.repeat(jnp.eye(B_HEADS, dtype=BF16), B_DV, axis=1)
    pos = jnp.arange(t)
    tri = (pos[:, None] >= pos[None, :]).astype(BF16)
    triu = (pos[:, None] <= pos[None, :]).astype(BF16)
    blk = lambda lg: pos >> lg
    off = jnp.stack([((blk(lg)[:, None] - blk(lg)[None, :] == 1) & ((blk(lg)[:, None] & 1) == 1)).astype(BF16)
                     for lg in range(1, t.bit_length() - 1)])
    avg = jnp.full((B_DV, B_DV), 1.0 / B_DV, BF16)
    params = [w["w_conv"], w["a_log_r"], w["dt_r"], w["a_log_c"], w["dt_c"], w["g_o"], expand, tri, triu, off, avg]
    bb = min(bb, b)
    assert b % bb == 0 and s % t == 0
    return pl.pallas_call(
        functools.partial(_gdn_chunk_kernel, t=t, valid=valid, bb=bb),
        grid=(b // bb, s // t),
        in_specs=[
            pl.BlockSpec((bb, t, B_CONV_CH), lambda i, j: (i, j, 0)),
            pl.BlockSpec((bb, t, 2 * B_HEADS), lambda i, j: (i, j, 0)),
            pl.BlockSpec((bb, 1, 2 * B_HEADS, t), lambda i, j: (i, j, 0, 0)),
            pl.BlockSpec((bb, HALO, B_CONV_CH), lambda i, j: (i, 0, 0)),
            pl.BlockSpec((bb, B_HEADS, B_DK, B_DV), lambda i, j: (i, 0, 0, 0)),
        ] + [full(p) for p in params],
        out_specs=[
            pl.BlockSpec((bb, t, B_WIDTH), lambda i, j: (i, j, 0)),
            pl.BlockSpec((bb, B_HEADS, B_DK, B_DV), lambda i, j: (i, 0, 0, 0)),
        ],
        out_shape=[
            jax.ShapeDtypeStruct((b, s, B_WIDTH), F32),
            jax.ShapeDtypeStruct((b, B_HEADS, B_DK, B_DV), F32),
        ],
        scratch_shapes=[
            pltpu.VMEM((bb, t + HALO, B_CONV_CH), F32),
            pltpu.VMEM((bb, t, B_CONV_CH), F32),
            pltpu.VMEM((bb, t, B_WIDTH), F32),
            pltpu.VMEM((bb, t, B_WIDTH), F32),
            pltpu.VMEM((bb, B_HEADS, B_DK, B_DV), F32),
            pltpu.VMEM((bb * B_HEADS, t, t), BF16),
            pltpu.VMEM((bb * B_HEADS, t, t), BF16),
            pltpu.VMEM((bb * B_HEADS, t, t), BF16),
            pltpu.VMEM((bb * B_HEADS, t, t), BF16),
        ],
        compiler_params=_cparams(("parallel", "arbitrary")),
        name="gdn_chunk",
    )(qkv, ab, abt, conv0, st0, *params)


def _pad_heads(wm, n_in):
    d = wm.shape[-1]
    return jnp.pad(wm, ((0, 0), (0, 0), (0, SLOT - d))).reshape(n_in, A_HEADS * SLOT)


def _mla_weights(a_norm, a_w_in, a_g_qa, a_w_uq, a_g_kv, a_w_uk, a_w_uv, a_g_q, a_g_k, a_w_o):
    o1 = A_Q_LORA
    o2 = o1 + A_KV_LORA
    o3 = o2 + A_ROPE
    kslot = jnp.pad(a_w_in[:, o2:o3], ((0, 0), (A_NOPE, SLOT - A_QK)))
    w_in = jnp.concatenate([a_w_in[:, :o2], a_w_in[:, o3:], kslot], 1).astype(BF16)
    pad_gain = lambda g: jnp.pad(g, (0, SLOT - A_QK)).reshape(1, SLOT)
    return {
        "a_norm": a_norm.reshape(1, D_MODEL),
        "w_in": w_in,
        "g_qa": a_g_qa.reshape(1, A_Q_LORA),
        "w_uq": _pad_heads(a_w_uq.reshape(A_Q_LORA, A_HEADS, A_QK), A_Q_LORA).astype(BF16),
        "g_kv": a_g_kv.reshape(1, A_KV_LORA),
        "g_q": pad_gain(a_g_q * (A_QK ** -0.5)),
        "g_k": pad_gain(a_g_k),
        "w_uk": _pad_heads(a_w_uk, A_KV_LORA).astype(BF16),
        "w_ukt": jnp.transpose(a_w_uk, (1, 2, 0)).astype(BF16),
        "w_uv": a_w_uv.reshape(A_KV_LORA, A_WIDTH).astype(BF16),
        "w_uv_h": jnp.transpose(a_w_uv, (1, 0, 2)).astype(BF16),
        "w_o": a_w_o.astype(BF16),
    }


def _gdn_weights(b_norm, b_w_in, b_w_conv, b_a_log, b_dt_bias, b_g_o, b_w_o):
    o2 = B_CONV_CH + B_WIDTH
    w_in = jnp.pad(b_w_in, ((0, 0), (0, SLOT - 2 * B_HEADS))).astype(BF16)
    return {
        "b_norm": b_norm.reshape(1, D_MODEL),
        "w_in": w_in,
        "w_abt": b_w_in[:, o2:].T.astype(BF16),
        "w_conv": b_w_conv,
        "a_log_r": b_a_log.reshape(1, B_HEADS),
        "dt_r": b_dt_bias.reshape(1, B_HEADS),
        "a_log_c": b_a_log.reshape(B_HEADS, 1),
        "dt_c": b_dt_bias.reshape(B_HEADS, 1),
        "g_o": b_g_o.reshape(1, B_DV),
        "w_o": b_w_o.astype(BF16),
    }


def _mla_layer(xp, xs, cache_latent, cache_krope, la, page_table, w, tabs_p, tabs_s):
    b, s, _ = xp.shape
    n, t, _ = xs.shape
    q, c_p, kpe_p, z, k, v = _mla_project(xp, tabs_p, w, with_kv=True, tm=256, q_dtype=BF16)
    bound = A_QK * BF16_NORM_SLACK * jnp.max(jnp.abs(w["g_q"])) * jnp.max(jnp.abs(w["g_k"]))
    o = _flash_attention(q, k, v, bound, tq=512, tk=512)
    yp = _gated_out(xp.reshape(b * s, D_MODEL), o.reshape(b * s, A_WIDTH), z.reshape(b * s, A_WIDTH), w["w_o"], tm=512)

    q, c_s, kpe_s, z = _mla_project(xs.reshape(1, n * t, D_MODEL), tabs_s, w, with_kv=False, tm=256, q_dtype=F32)
    c_s = c_s.reshape(n, t, A_KV_LORA)
    kpe_s = kpe_s.reshape(n, t, A_ROPE)
    q8 = jnp.pad(q.reshape(n, t, A_QPAD), ((0, 0), (0, 8 - t), (0, 0)))
    c_new = jnp.pad(c_s, ((0, 0), (0, PAGE - t), (0, 0)))
    kpe_new_t = jnp.pad(jnp.swapaxes(kpe_s, 1, 2), ((0, 0), (0, 0), (0, PAGE - t)))
    o = _paged_attention(q8, c_new, kpe_new_t, cache_latent, jnp.swapaxes(cache_krope, 2, 3), la, page_table, w,
                         ppb=16)
    ys = _gated_out(xs.reshape(n * t, D_MODEL), o[:, :t].reshape(n * t, A_WIDTH), z.reshape(n * t, A_WIDTH),
                    w["w_o"], tm=512)
    return yp.reshape(b, s, D_MODEL), ys.reshape(n, t, D_MODEL), c_p, kpe_p, c_s, kpe_s


def _gdn_layer(xp, xs, state_conv, state_ssm, w):
    b, s, _ = xp.shape
    n, t, _ = xs.shape
    tc = min(256, s)
    qkv, z, ab, abt = _gdn_project(xp.reshape(b * s, D_MODEL), w, tm=256)
    qkv = qkv.reshape(b, s, B_CONV_CH)
    abt = abt.reshape(2 * B_HEADS, b, s // tc, tc).transpose(1, 2, 0, 3)
    o, st_p = _gdn_chunked(qkv, ab.reshape(b, s, 2 * B_HEADS), abt,
                           jnp.zeros((b, HALO, B_CONV_CH), F32), jnp.zeros((b, B_HEADS, B_DK, B_DV), F32),
                           w, t=tc, valid=tc, bb=1)
    yp = _gated_out(xp.reshape(b * s, D_MODEL), o.reshape(b * s, B_WIDTH), z, w["w_o"], tm=512)
    conv_p = qkv[:, s - (B_CONV - 1):]

    qkv, z, ab, abt = _gdn_project(xs.reshape(n * t, D_MODEL), w, tm=256)
    qkv = qkv.reshape(n, t, B_CONV_CH)
    pad_t = lambda a: jnp.pad(a, ((0, 0), (0, 8 - t), (0, 0)))
    abt = jnp.pad(abt.reshape(2 * B_HEADS, n, t).transpose(1, 0, 2), ((0, 0), (0, 0), (0, 8 - t)))
    conv0 = jnp.pad(state_conv, ((0, 0), (HALO - (B_CONV - 1), 0), (0, 0)))
    o, st_s = _gdn_chunked(pad_t(qkv), pad_t(ab.reshape(n, t, 2 * B_HEADS)), abt.reshape(n, 1, 2 * B_HEADS, 8),
                           conv0, state_ssm, w, t=8, valid=t, bb=4)
    ys = _gated_out(xs.reshape(n * t, D_MODEL), o[:, :t].reshape(n * t, B_WIDTH), z, w["w_o"], tm=512)
    conv_s = jnp.concatenate([state_conv, qkv], 1)[:, -(B_CONV - 1):]
    return yp.reshape(b, s, D_MODEL), ys.reshape(n, t, D_MODEL), conv_p, st_p, conv_s, st_s


def kernel(x_prompt, x_sample, cache_latent, cache_krope, page_table, state_conv, state_ssm,
           a_norm, a_w_in, a_g_qa, a_w_uq, a_g_kv, a_w_uk, a_w_uv, a_g_q, a_g_k, a_w_o,
           b_norm, b_w_in, b_w_conv, b_a_log, b_dt_bias, b_g_o, b_w_o):
    s = x_prompt.shape[1]
    n, t, _ = x_sample.shape
    past = page_table.shape[1] * PAGE
    p_pad = -(-(s + t) // SLOT) * SLOT
    pos = jnp.concatenate([jnp.arange(s), past + jnp.arange(t), jnp.zeros((p_pad - s - t,), jnp.int32)]).astype(F32)
    tabs = _rope_tables(pos)
    tabs_p = tuple(tb[:s] for tb in tabs)
    tabs_s = tuple(jnp.tile(tb[s:s + t], (n, 1)) for tb in tabs)

    wa = _mla_weights(a_norm[0], a_w_in[0], a_g_qa[0], a_w_uq[0], a_g_kv[0], a_w_uk[0], a_w_uv[0], a_g_q[0],
                      a_g_k[0], a_w_o[0])
    wb = _gdn_weights(b_norm[0], b_w_in[0], b_w_conv[0], b_a_log[0], b_dt_bias[0], b_g_o[0], b_w_o[0])

    xp, xs, lat_p, kpe_p, lat_s, kpe_s = _mla_layer(x_prompt, x_sample, cache_latent, cache_krope, 0, page_table,
                                                    wa, tabs_p, tabs_s)
    xp, xs, conv_p, ssm_p, conv_s, ssm_s = _gdn_layer(xp, xs, state_conv[0], state_ssm[0], wb)
    return (xp, xs, lat_p[None], kpe_p[None], lat_s[None], kpe_s[None],
            conv_p[None], ssm_p[None], conv_s[None], ssm_s[None])
```

```python
import functools

import jax
import jax.numpy as jnp
from jax import lax
from jax.experimental import pallas as pl
from jax.experimental.pallas import tpu as pltpu

F32 = jnp.float32
BF16 = jnp.bfloat16
EPS = 1e-6

D_MODEL = 1024
PAGE = 128
A_HEADS = 8
A_NOPE = 64
A_ROPE = 32
A_QK = A_NOPE + A_ROPE
A_V = 64
A_Q_LORA = 384
A_KV_LORA = 256
A_WIDTH = A_HEADS * A_V
ROPE_THETA = 10000.0
SLOT = 128
A_QPAD = A_HEADS * SLOT
B_HEADS = 8
B_DK = 64
B_DV = 64
B_WIDTH = B_HEADS * B_DV
B_CONV = 4
B_CONV_CH = 2 * B_HEADS * B_DK + B_WIDTH
HALO = 8
BF16_ROWS = 16

VMEM_LIMIT = 56 * 1024 * 1024
FIXED_SHIFT_MAX = 30.0
BF16_NORM_SLACK = 1.02
PAGED_SLOTS = 3
PAGED_PARTS = 2


def _cparams(sem):
    return pltpu.CompilerParams(dimension_semantics=sem, vmem_limit_bytes=VMEM_LIMIT)


def _dot(a, b):
    return jnp.dot(a, b, preferred_element_type=F32)


def _dot_nt(a, b):
    return lax.dot_general(a, b, (((1,), (1,)), ((), ())), preferred_element_type=F32)


def _dot_tn(a, b):
    return lax.dot_general(a, b, (((0,), (0,)), ((), ())), preferred_element_type=F32)


def _dot_sel(sel_bf16, x, *, sel_first):
    x1 = x.astype(BF16)
    r1 = x - x1.astype(F32)
    x2 = r1.astype(BF16)
    x3 = (r1 - x2.astype(F32)).astype(BF16)
    if sel_first:
        return _dot(sel_bf16, x1) + _dot(sel_bf16, x2) + _dot(sel_bf16, x3)
    return _dot(x1, sel_bf16) + _dot(x2, sel_bf16) + _dot(x3, sel_bf16)


def _rms(x, g):
    return x * lax.rsqrt(jnp.mean(x * x, -1, keepdims=True) + EPS) * g


def _silu(x):
    return x * jax.nn.sigmoid(x)


def _softplus(x):
    return jnp.maximum(x, 0.0) + jnp.log(1.0 + jnp.exp(-jnp.abs(x)))


def _rope_table_kernel(pos_ref, inv_ref, cos_ref, sin_ref):
    ang = inv_ref[...] * pos_ref[...]
    cos_ref[...] = jnp.cos(ang)
    sin_ref[...] = jnp.sin(ang)


def _rope_tables(pos):
    half = A_ROPE // 2
    p = pos.shape[0]
    inv = (ROPE_THETA ** (-jnp.arange(half, dtype=F32) / half)).reshape(half, 1)
    cos_t, sin_t = pl.pallas_call(
        _rope_table_kernel,
        out_shape=(jax.ShapeDtypeStruct((half, p), F32),) * 2,
        name="rope_tables",
    )(pos.reshape(1, p), inv)
    cos = cos_t.T
    sin = sin_t.T
    one = jnp.ones((p, A_NOPE), F32)
    zn = jnp.zeros((p, A_NOPE), F32)
    zh = jnp.zeros((p, half), F32)
    zp = jnp.zeros((p, SLOT - A_QK), F32)
    tc = jnp.concatenate([one, cos, cos, zp], 1)
    ts1 = jnp.concatenate([zn, -sin, zh, zp], 1)
    ts2 = jnp.concatenate([zn, zh, sin, zp], 1)
    return tc, ts1, ts2


def _mla_proj_kernel(*refs, with_kv):
    (x_ref, tc_ref, ts1_ref, ts2_ref, gn_ref, win_ref, gqa_ref, wuq_ref, gkv_ref, gq_ref, ones_ref) = refs[:11]
    if with_kv:
        wuk_ref, gk_ref, wuv_ref = refs[11:14]
        q_out, c_out, kpe_out, z_out, k_out, v_out = refs[14:]
    else:
        q_out, c_out, kpe_out, z_out = refs[11:]
    x = x_ref[0]
    h = _rms(x, gn_ref[...])
    proj = _dot(h.astype(BF16), win_ref[...])
    o1 = A_Q_LORA
    o2 = o1 + A_KV_LORA
    o3 = o2 + A_WIDTH
    qa = _rms(proj[:, :o1], gqa_ref[...])
    c = _rms(proj[:, o1:o2], gkv_ref[...])
    z_out[0] = proj[:, o2:o3].astype(z_out.dtype)
    c_out[0] = c
    q = _dot(qa.astype(BF16), wuq_ref[...])
    tc = tc_ref[...]
    ts1 = ts1_ref[...]
    ts2 = ts2_ref[...]

    def rope(s):
        return s * tc + pltpu.roll(s, SLOT - A_ROPE // 2, 1) * ts1 + pltpu.roll(s, A_ROPE // 2, 1) * ts2

    ones = ones_ref[...]

    def head_norm(s, g):
        s2 = s * s
        s2h = s2.astype(BF16)
        ms = (_dot(s2h, ones) + _dot((s2 - s2h.astype(F32)).astype(BF16), ones)) * (1.0 / A_QK)
        return s * lax.rsqrt(ms + EPS) * g

    kslot = rope(proj[:, o3:o3 + SLOT])
    kpe_out[0] = kslot[:, A_NOPE:A_QK]
    gq = gq_ref[...]
    for hd in range(A_HEADS):
        sl = slice(hd * SLOT, (hd + 1) * SLOT)
        q_out[0, :, sl] = head_norm(rope(q[:, sl]), gq).astype(q_out.dtype)
    if with_kv:
        cb = c.astype(BF16)
        kn = _dot(cb, wuk_ref[...])
        v_out[0] = _dot(cb, wuv_ref[...]).astype(v_out.dtype)
        gk = gk_ref[...]
        for hd in range(A_HEADS):
            sl = slice(hd * SLOT, (hd + 1) * SLOT)
            k_out[0, :, sl] = head_norm(kn[:, sl] + kslot, gk).astype(k_out.dtype)


def _mla_project(x, tabs, w, *, with_kv, tm, q_dtype):
    b, s, _ = x.shape
    tm = min(tm, s)
    grid = (s // tm, b)
    full = lambda a: pl.BlockSpec(a.shape, lambda i, j: (0,) * a.ndim)
    row = lambda n: pl.BlockSpec((1, tm, n), lambda i, j: (j, i, 0))
    tab = pl.BlockSpec((tm, SLOT), lambda i, j: (i, 0))
    params = [w["a_norm"], w["w_in"], w["g_qa"], w["w_uq"], w["g_kv"], w["g_q"], jnp.ones((SLOT, SLOT), BF16)]
    out_shape = [
        jax.ShapeDtypeStruct((b, s, A_QPAD), q_dtype),
        jax.ShapeDtypeStruct((b, s, A_KV_LORA), F32),
        jax.ShapeDtypeStruct((b, s, A_ROPE), F32),
        jax.ShapeDtypeStruct((b, s, A_WIDTH), q_dtype),
    ]
    out_specs = [row(A_QPAD), row(A_KV_LORA), row(A_ROPE), row(A_WIDTH)]
    if with_kv:
        params += [w["w_uk"], w["g_k"], w["w_uv"]]
        out_shape += [jax.ShapeDtypeStruct((b, s, A_QPAD), BF16), jax.ShapeDtypeStruct((b, s, A_WIDTH), BF16)]
        out_specs += [row(A_QPAD), row(A_WIDTH)]
    return pl.pallas_call(
        functools.partial(_mla_proj_kernel, with_kv=with_kv),
        grid=grid,
        in_specs=[row(D_MODEL), tab, tab, tab] + [full(p) for p in params],
        out_specs=out_specs,
        out_shape=out_shape,
        compiler_params=_cparams(("parallel", "parallel")),
        name="mla_proj_kv" if with_kv else "mla_proj",
    )(x, *tabs, *params)


def _flash_kernel(qi_ref, ki_ref, dv_ref, fx_ref, q_ref, k_ref, v_ref, sh_ref, o_ref, m_ref, l_ref, acc_ref,
                  *, tq, tk):
    del qi_ref
    p = pl.program_id(1)
    ki = ki_ref[p]
    dv = dv_ref[p]
    fixed = fx_ref[0] == 1

    @pl.when(ki == 0)
    def _():
        m_ref[...] = jnp.full(m_ref.shape, -jnp.inf, F32)
        l_ref[...] = jnp.zeros(l_ref.shape, F32)
        acc_ref[...] = jnp.zeros(acc_ref.shape, F32)

    low = lax.broadcasted_iota(jnp.int32, (tq, SLOT), 1) < A_V

    def step(d, fixed_shift):
        diag = d is not None
        if diag:
            keep = (lax.broadcasted_iota(jnp.int32, (tq, tk), 1) + d * tk
                    <= lax.broadcasted_iota(jnp.int32, (tq, tk), 0))
        if fixed_shift:
            shift = sh_ref[:, :1]
        for j in range(A_HEADS // 2):
            pv = []
            al = []
            vpair = v_ref[0, :, j * SLOT:(j + 1) * SLOT]
            for e in range(2):
                hd = 2 * j + e
                sl = slice(hd * SLOT, (hd + 1) * SLOT)
                s = _dot_nt(q_ref[0, :, sl], k_ref[0, :, sl])
                if diag:
                    s = jnp.where(keep, s, -jnp.inf)
                if fixed_shift:
                    pr = jnp.exp(s - shift)
                    l_ref[hd] = l_ref[hd] + jnp.sum(pr, -1, keepdims=True)
                else:
                    m_prev = m_ref[hd]
                    m_new = jnp.maximum(m_prev, jnp.max(s, -1, keepdims=True))
                    alpha = jnp.exp(m_prev - m_new)
                    pr = jnp.exp(s - m_new[:, :1])
                    l_ref[hd] = alpha * l_ref[hd] + jnp.sum(pr, -1, keepdims=True)
                    m_ref[hd] = m_new
                    al.append(alpha)
                pv.append(_dot(pr.astype(BF16), vpair))
            sl = slice(j * SLOT, (j + 1) * SLOT)
            if fixed_shift:
                acc_ref[:, sl] = acc_ref[:, sl] + jnp.where(low, pv[0], pv[1])
            else:
                acc_ref[:, sl] = jnp.where(low, al[0], al[1]) * acc_ref[:, sl] + jnp.where(low, pv[0], pv[1])

    def finish():
        for j in range(A_HEADS // 2):
            sl = slice(j * SLOT, (j + 1) * SLOT)
            linv = jnp.where(low, 1.0 / l_ref[2 * j], 1.0 / l_ref[2 * j + 1])
            o_ref[0, :, sl] = (acc_ref[:, sl] * linv).astype(o_ref.dtype)

    n_diag = tq // tk
    for fs in (True, False):
        mode = fixed if fs else jnp.logical_not(fixed)

        @pl.when(jnp.logical_and(mode, dv < 0))
        def _():
            step(None, fs)

        for d in range(n_diag):
            @pl.when(jnp.logical_and(mode, dv == d))
            def _():
                step(d, fs)
                if d == n_diag - 1:
                    finish()


def _flash_attention(q, k, v, bound, *, tq, tk):
    b, s, _ = q.shape
    tq = min(tq, s)
    tk = min(tk, tq)
    assert s % tq == 0 and tq % tk == 0
    nq = s // tq
    r = tq // tk
    pairs = [(i, j, j - r * i if j >= r * i else -1) for i in range(nq) for j in range(r * (i + 1))]
    qi = jnp.asarray([pr[0] for pr in pairs], jnp.int32)
    ki = jnp.asarray([pr[1] for pr in pairs], jnp.int32)
    dv = jnp.asarray([pr[2] for pr in pairs], jnp.int32)
    fx = (bound <= FIXED_SHIFT_MAX).astype(jnp.int32).reshape(1)
    sh = jnp.full((1, SLOT), bound, F32)
    grid_spec = pltpu.PrefetchScalarGridSpec(
        num_scalar_prefetch=4,
        grid=(b, len(pairs)),
        in_specs=[
            pl.BlockSpec((1, tq, A_QPAD), lambda bi, p, qt, kt, dt, fx: (bi, qt[p], 0)),
            pl.BlockSpec((1, tk, A_QPAD), lambda bi, p, qt, kt, dt, fx: (bi, kt[p], 0)),
            pl.BlockSpec((1, tk, A_WIDTH), lambda bi, p, qt, kt, dt, fx: (bi, kt[p], 0)),
            pl.BlockSpec((1, SLOT), lambda bi, p, qt, kt, dt, fx: (0, 0)),
        ],
        out_specs=pl.BlockSpec((1, tq, A_WIDTH), lambda bi, p, qt, kt, dt, fx: (bi, qt[p], 0)),
        scratch_shapes=[
            pltpu.VMEM((A_HEADS, tq, SLOT), F32),
            pltpu.VMEM((A_HEADS, tq, SLOT), F32),
            pltpu.VMEM((tq, A_WIDTH), F32),
        ],
    )
    return pl.pallas_call(
        functools.partial(_flash_kernel, tq=tq, tk=tk),
        grid_spec=grid_spec,
        out_shape=jax.ShapeDtypeStruct((b, s, A_WIDTH), BF16),
        compiler_params=_cparams(("parallel", "arbitrary")),
        name="mla_flash",
    )(qi, ki, dv, fx, q, k, v, sh)


def _paged_kernel(pt_ref, q_ref, cn_ref, kn_ref, gk_ref, wukt_ref, wuv_ref, lat_hbm, kr_hbm, o_ref,
                  wq_ref, qr_ref, m_ref, l_ref, acc_ref, lat_buf, kr_buf, sem, *, la, ppb, n_blocks):
    smp = pl.program_id(0)
    blk = pl.program_id(1)
    rows = A_HEADS * 8
    hn = A_HEADS * A_NOPE

    step = smp * n_blocks + blk
    n_steps = pl.num_programs(0) * n_blocks
    ahead = PAGED_SLOTS - 1

    def page_copies(g):
        s_ = g // n_blocks
        b_ = g % n_blocks
        slot_ = g % PAGED_SLOTS
        out = []
        for i in range(ppb):
            page = pt_ref[s_, b_ * ppb + i]
            out.append(pltpu.make_async_copy(lat_hbm.at[la, page], lat_buf.at[slot_, pl.ds(i * PAGE, PAGE)],
                                             sem.at[slot_]))
            out.append(pltpu.make_async_copy(kr_hbm.at[la, page], kr_buf.at[slot_, :, pl.ds(i * PAGE, PAGE)],
                                             sem.at[slot_]))
        return out

    for d in range(ahead):
        @pl.when(jnp.logical_and(step == 0, d < n_steps))
        def _():
            for cp in page_copies(step + d):
                cp.start()

    @pl.when(step + ahead < n_steps)
    def _():
        for cp in page_copies(step + ahead):
            cp.start()

    for cp in page_copies(step):
        cp.wait()
    slot = step % PAGED_SLOTS

    @pl.when(blk == 0)
    def _():
        m_ref[...] = jnp.full(m_ref.shape, -jnp.inf, F32)
        l_ref[...] = jnp.zeros(l_ref.shape, F32)
        acc_ref[...] = jnp.zeros(acc_ref.shape, F32)
        gk = gk_ref[...]
        for hd in range(A_HEADS):
            qs = q_ref[0, :, hd * SLOT:(hd + 1) * SLOT] * gk
            wq_ref[hd * A_NOPE:(hd + 1) * A_NOPE, :] = wukt_ref[hd]
            wq_ref[hn + hd * 8:hn + (hd + 1) * 8, :] = _dot(qs[:, :A_NOPE].astype(BF16), wukt_ref[hd]).astype(BF16)
            qr_ref[hd * 8:(hd + 1) * 8, :] = qs[:, A_NOPE:A_QK].astype(BF16)

    def scores(cb, krt):
        nk = cb.shape[0]
        res = _dot_nt(wq_ref[...], cb)
        ssq = []
        for hd in range(A_HEADS):
            kh = res[hd * A_NOPE:(hd + 1) * A_NOPE]
            ssq.append(jnp.broadcast_to(jnp.sum(kh * kh, 0, keepdims=True), (8, nk)))
        ssq = jnp.concatenate(ssq, 0) + jnp.sum(krt * krt, 0, keepdims=True)
        s = res[hn:] + _dot(qr_ref[...], krt.astype(BF16))
        return s * lax.rsqrt(ssq * (1.0 / A_QK) + EPS)

    def block(cbs, krts, new):
        sts = [scores(cb, krt) for cb, krt in zip(cbs, krts)]
        if new:
            nk = cbs[0].shape[0]
            tok = lax.broadcasted_iota(jnp.int32, (rows, nk), 0) & 7
            key = lax.broadcasted_iota(jnp.int32, (rows, nk), 1)
            sts = [jnp.where(key <= tok, st, -jnp.inf) for st in sts]
        ms = [jnp.max(st, -1, keepdims=True) for st in sts]
        prs = [jnp.exp(st - m) for st, m in zip(sts, ms)]
        ls = [jnp.sum(pr, -1, keepdims=True) for pr in prs]
        pvs = [_dot(pr.astype(BF16), cb) for pr, cb in zip(prs, cbs)]
        m_prev = m_ref[...]
        m_new = m_prev
        for m in ms:
            m_new = jnp.maximum(m_new, m)
        alpha = jnp.exp(m_prev - m_new)
        l_new = alpha * l_ref[...]
        acc = jnp.concatenate([alpha, alpha], 1) * acc_ref[...]
        for m, l, pv in zip(ms, ls, pvs):
            wgt = jnp.exp(m - m_new)
            l_new = l_new + wgt * l
            acc = acc + jnp.concatenate([wgt, wgt], 1) * pv
        m_ref[...] = m_new
        l_ref[...] = l_new
        acc_ref[...] = acc

    sub = max(ppb // PAGED_PARTS, 1) * PAGE
    block([lat_buf[slot, i:i + sub].astype(BF16) for i in range(0, ppb * PAGE, sub)],
          [kr_buf[slot, :, i:i + sub] for i in range(0, ppb * PAGE, sub)], False)

    @pl.when(blk == n_blocks - 1)
    def _():
        block([cn_ref[0].astype(BF16)], [kn_ref[0]], True)
        o_lat = (acc_ref[...] * (1.0 / l_ref[...][:, :1])).astype(BF16)
        for hd in range(A_HEADS):
            o_ref[0, :, hd * A_V:(hd + 1) * A_V] = _dot(o_lat[hd * 8:(hd + 1) * 8], wuv_ref[hd])


def _paged_attention(q, c_new, kpe_new_t, cache_latent, cache_krope_t, la, page_table, w, *, ppb):
    n = q.shape[0]
    n_pages = page_table.shape[1]
    ppb = min(ppb, n_pages)
    n_blocks = n_pages // ppb
    full = lambda a: pl.BlockSpec(a.shape, lambda i, j, pt: (0,) * a.ndim)
    rows = A_HEADS * 8
    params = [w["g_k"], w["w_ukt"], w["w_uv_h"]]

    hbm = pl.BlockSpec(memory_space=pl.ANY)
    grid_spec = pltpu.PrefetchScalarGridSpec(
        num_scalar_prefetch=1,
        grid=(n, n_blocks),
        in_specs=[
            pl.BlockSpec((1, 8, A_QPAD), lambda s, j, pt: (s, 0, 0)),
            pl.BlockSpec((1, PAGE, A_KV_LORA), lambda s, j, pt: (s, 0, 0)),
            pl.BlockSpec((1, A_ROPE, PAGE), lambda s, j, pt: (s, 0, 0)),
        ] + [full(p) for p in params] + [hbm, hbm],
        out_specs=pl.BlockSpec((1, 8, A_WIDTH), lambda s, j, pt: (s, 0, 0)),
        scratch_shapes=[
            pltpu.VMEM((A_HEADS * A_NOPE + rows, A_KV_LORA), BF16),
            pltpu.VMEM((rows, A_ROPE), BF16),
            pltpu.VMEM((rows, SLOT), F32),
            pltpu.VMEM((rows, SLOT), F32),
            pltpu.VMEM((rows, A_KV_LORA), F32),
            pltpu.VMEM((PAGED_SLOTS, ppb * PAGE, A_KV_LORA), F32),
            pltpu.VMEM((PAGED_SLOTS, A_ROPE, ppb * PAGE), F32),
            pltpu.SemaphoreType.DMA((PAGED_SLOTS,)),
        ],
    )
    return pl.pallas_call(
        functools.partial(_paged_kernel, la=la, ppb=ppb, n_blocks=n_blocks),
        grid_spec=grid_spec,
        out_shape=jax.ShapeDtypeStruct((n, 8, A_WIDTH), F32),
        compiler_params=_cparams(("arbitrary", "arbitrary")),
        name="mla_paged",
    )(page_table, q, c_new, kpe_new_t, *params, cache_latent, cache_krope_t)


def _gated_out_kernel(x_ref, o_ref, z_ref, w_ref, y_ref):
    gated = o_ref[...].astype(F32) * _silu(z_ref[...].astype(F32))
    y_ref[...] = x_ref[...] + _dot(gated.astype(BF16), w_ref[...])


def _gated_out(x, o, z, w_o, *, tm):
    r = x.shape[0]
    tm = min(tm, r)
    row = lambda n: pl.BlockSpec((tm, n), lambda i: (i, 0))
    return pl.pallas_call(
        _gated_out_kernel,
        grid=(r // tm,),
        in_specs=[row(D_MODEL), row(o.shape[1]), row(z.shape[1]), pl.BlockSpec(w_o.shape, lambda i: (0, 0))],
        out_specs=row(D_MODEL),
        out_shape=jax.ShapeDtypeStruct((r, D_MODEL), F32),
        compiler_params=_cparams(("parallel",)),
        name="gated_out",
    )(x, o, z, w_o)


def _gdn_proj_kernel(x_ref, gn_ref, w_ref, wabt_ref, qkv_out, z_out, ab_out, abt_out):
    hb = _rms(x_ref[...], gn_ref[...]).astype(BF16)
    proj = _dot(hb, w_ref[...])
    o1 = B_CONV_CH
    o2 = o1 + B_WIDTH
    qkv_out[...] = proj[:, :o1]
    z_out[...] = proj[:, o1:o2].astype(z_out.dtype)
    ab_out[...] = proj[:, o2:o2 + 2 * B_HEADS]
    abt_out[...] = _dot_nt(wabt_ref[...], hb)


def _gdn_project(x, w, *, tm, z_dtype):
    r = x.shape[0]
    tm = min(tm, r)
    row = lambda n: pl.BlockSpec((tm, n), lambda i: (i, 0))
    full = lambda a: pl.BlockSpec(a.shape, lambda i: (0,) * a.ndim)
    params = [w["b_norm"], w["w_in"], w["w_abt"]]
    return pl.pallas_call(
        _gdn_proj_kernel,
        grid=(r // tm,),
        in_specs=[row(D_MODEL)] + [full(p) for p in params],
        out_specs=[row(B_CONV_CH), row(B_WIDTH), row(2 * B_HEADS), pl.BlockSpec((2 * B_HEADS, tm), lambda i: (0, i))],
        out_shape=[
            jax.ShapeDtypeStruct((r, B_CONV_CH), F32),
            jax.ShapeDtypeStruct((r, B_WIDTH), z_dtype),
            jax.ShapeDtypeStruct((r, 2 * B_HEADS), F32),
            jax.ShapeDtypeStruct((2 * B_HEADS, r), F32),
        ],
        compiler_params=_cparams(("parallel",)),
        name="gdn_proj",
    )(x, *params)


def _gdn_chunk_kernel(qkv_ref, ab_ref, abt_ref, conv0_ref, st0_ref, wc_ref, alr_ref, dtr_ref, alc_ref, dtc_ref,
                      go_ref, exp_ref, tri_ref, triu_ref, off_ref, avg_ref, o_ref, st_out, ext_ref, feat_ref, gcx_ref,
                      bx_ref, st_ref, lm_ref, inv_ref, t1_ref, in_ref, *, t, valid, bb):
    si = pl.program_id(1)
    hk = B_HEADS * B_DK
    units = [(bi, hd) for bi in range(bb) for hd in range(B_HEADS)]
    qsl = lambda hd: slice(hd * B_DK, (hd + 1) * B_DK)
    ksl = lambda hd: slice(hk + hd * B_DK, hk + (hd + 1) * B_DK)
    vsl = lambda hd: slice(2 * hk + hd * B_DV, 2 * hk + (hd + 1) * B_DV)

    @pl.when(si == 0)
    def _():
        ext_ref[:, 0:HALO] = conv0_ref[...]
        st_ref[...] = st0_ref[...]

    gcts = []
    for bi in range(bb):
        ext_ref[bi, HALO:HALO + t] = qkv_ref[bi]
        conv = ext_ref[bi, HALO - 3:HALO - 3 + t] * wc_ref[0:1]
        for j in range(1, B_CONV):
            conv = conv + ext_ref[bi, HALO - 3 + j:HALO - 3 + j + t] * wc_ref[j:j + 1]
        feat_ref[bi] = _silu(conv)
        ext_ref[bi, 0:HALO] = ext_ref[bi, t:t + HALO]

        ab = ab_ref[bi]
        g = -jnp.exp(alr_ref[...]) * _softplus(ab[:, :B_HEADS] + dtr_ref[...])
        beta = jax.nn.sigmoid(ab[:, B_HEADS:])
        if valid < t:
            live = lax.broadcasted_iota(jnp.int32, (t, B_HEADS), 0) < valid
            g = jnp.where(live, g, 0.0)
            beta = jnp.where(live, beta, 0.0)
        bx_ref[bi] = _dot_sel(exp_ref[...], beta, sel_first=False)
        gcx_ref[bi] = _dot_sel(tri_ref[...], _dot_sel(exp_ref[...], g, sel_first=False), sel_first=True)
        gt = -jnp.exp(alc_ref[...]) * _softplus(abt_ref[bi, 0][:B_HEADS] + dtc_ref[...])
        if valid < t:
            gt = jnp.where(lax.broadcasted_iota(jnp.int32, (B_HEADS, t), 1) < valid, gt, 0.0)
        gcts.append(_dot_sel(triu_ref[...], gt, sel_first=False))

    ii = lax.broadcasted_iota(jnp.int32, (t, t), 0)
    jj = lax.broadcasted_iota(jnp.int32, (t, t), 1)
    incl = ii >= jj
    strict = ii > jj
    eye = (ii == jj).astype(F32)
    pair = (ii >> 1) == (jj >> 1)

    def cols(x):
        return x[:, :t] if t <= B_DK else jnp.tile(x, (1, t // B_DK))

    for u, (bi, hd) in enumerate(units):
        qh = feat_ref[bi, :, qsl(hd)]
        kh = feat_ref[bi, :, ksl(hd)]
        qh = qh * lax.rsqrt(jnp.sum(qh * qh, -1, keepdims=True) + EPS) * (B_DK ** -0.5)
        kh = kh * lax.rsqrt(jnp.sum(kh * kh, -1, keepdims=True) + EPS)
        feat_ref[bi, :, qsl(hd)] = qh
        feat_ref[bi, :, ksl(hd)] = kh
        decay = jnp.exp(jnp.where(incl, cols(gcx_ref[bi, :, qsl(hd)]) - gcts[bi][hd:hd + 1, :], -jnp.inf))
        khb = kh.astype(BF16)
        lmat = jnp.where(strict, _dot_nt((kh * bx_ref[bi, :, qsl(hd)]).astype(BF16), khb) * decay, 0.0)
        lm_ref[u] = lmat.astype(BF16)
        in_ref[u] = (_dot_nt(qh.astype(BF16), khb) * decay).astype(BF16)
        inv_ref[u] = (eye - jnp.where(pair, lmat, 0.0)).astype(BF16)

    for lv in range(off_ref.shape[0]):
        bsz = 2 << lv
        if bsz % BF16_ROWS == 0:
            rngs = [((2 * m + 1) * bsz, (2 * m + 2) * bsz) for m in range(t // (2 * bsz))]
            rows_of = lambda ref, i: jnp.concatenate([ref[i, a:b, :] for a, b in rngs], 0)
            zero = jnp.zeros((bsz, t), BF16)
            for u in range(len(units)):
                half = _dot(rows_of(lm_ref, u) * rows_of(off_ref, lv), inv_ref[u]).astype(BF16)
                pieces = []
                for m in range(len(rngs)):
                    pieces += [zero, half[m * bsz:(m + 1) * bsz]]
                t1_ref[u] = jnp.concatenate(pieces, 0)
            for u in range(len(units)):
                upd = _dot(rows_of(inv_ref, u), t1_ref[u]).astype(BF16)
                for m, (a, b) in enumerate(rngs):
                    inv_ref[u, a:b, :] = inv_ref[u, a:b, :] - upd[m * bsz:(m + 1) * bsz]
        else:
            for u in range(len(units)):
                t1_ref[u] = _dot(lm_ref[u] * off_ref[lv], inv_ref[u]).astype(BF16)
            for u in range(len(units)):
                inv_ref[u] = inv_ref[u] - _dot(inv_ref[u], t1_ref[u]).astype(BF16)

    go = go_ref[...]
    us, ws, egs = [], [], []
    for u, (bi, hd) in enumerate(units):
        invb = inv_ref[u]
        bc = bx_ref[bi, :, qsl(hd)]
        eg = jnp.exp(gcx_ref[bi, :, qsl(hd)])
        kb = feat_ref[bi, :, ksl(hd)] * bc
        us.append(_dot(invb, (feat_ref[bi, :, vsl(hd)] * bc).astype(BF16)))
        ws.append(_dot(invb, (kb * eg).astype(BF16)).astype(BF16))
        egs.append(eg)
    vns = []
    for u, (bi, hd) in enumerate(units):
        vns.append((us[u] - _dot(ws[u], st_ref[bi, hd].astype(BF16))).astype(BF16))
    for u, (bi, hd) in enumerate(units):
        st = st_ref[bi, hd]
        o = _dot((feat_ref[bi, :, qsl(hd)] * egs[u]).astype(BF16), st.astype(BF16)) + _dot(in_ref[u], vns[u])
        gcc = gcx_ref[bi, :, qsl(hd)]
        glast = gcc[t - 1:t, :]
        kdec = feat_ref[bi, :, ksl(hd)] * jnp.exp(glast - gcc)
        st_ref[bi, hd] = st * jnp.exp(glast) + _dot_tn(kdec.astype(BF16), vns[u])
        if t >= B_DV:
            o2 = o * o
            o2h = o2.astype(BF16)
            ms = _dot(o2h, avg_ref[...]) + _dot((o2 - o2h.astype(F32)).astype(BF16), avg_ref[...])
        else:
            ms = jnp.mean(o * o, -1, keepdims=True)
        o_ref[bi, :, qsl(hd)] = (o * lax.rsqrt(ms + EPS) * go).astype(o_ref.dtype)

    @pl.when(si == pl.num_programs(1) - 1)
    def _():
        st_out[...] = st_ref[...]


def _gdn_chunked(qkv, ab, abt, conv0, st0, w, *, t, valid, bb, o_dtype):
    b, s, _ = qkv.shape
    t = min(t, s)
    full = lambda a: pl.BlockSpec(a.shape, lambda i, j: (0,) * a.ndim)
    expand = jnp.repeat(jnp.eye(B_HEADS, dtype=BF16), B_DV, axis=1)
    pos = jnp.arange(t)
    tri = (pos[:, None] >= pos[None, :]).astype(BF16)
    triu = (pos[:, None] <= pos[None, :]).astype(BF16)
    blk = lambda lg: pos >> lg
    off = jnp.stack([((blk(lg)[:, None] - blk(lg)[None, :] == 1) & ((blk(lg)[:, None] & 1) == 1)).astype(BF16)
                     for lg in range(1, t.bit_length() - 1)])
    avg = jnp.full((B_DV, B_DV), 1.0 / B_DV, BF16)
    params = [w["w_conv"], w["a_log_r"], w["dt_r"], w["a_log_c"], w["dt_c"], w["g_o"], expand, tri, triu, off, avg]
    bb = min(bb, b)
    assert b % bb == 0 and s % t == 0
    return pl.pallas_call(
        functools.partial(_gdn_chunk_kernel, t=t, valid=valid, bb=bb),
        grid=(b // bb, s // t),
        in_specs=[
            pl.BlockSpec((bb, t, B_CONV_CH), lambda i, j: (i, j, 0)),
            pl.BlockSpec((bb, t, 2 * B_HEADS), lambda i, j: (i, j, 0)),
            pl.BlockSpec((bb, 1, 2 * B_HEADS, t), lambda i, j: (i, j, 0, 0)),
            pl.BlockSpec((bb, HALO, B_CONV_CH), lambda i, j: (i, 0, 0)),
            pl.BlockSpec((bb, B_HEADS, B_DK, B_DV), lambda i, j: (i, 0, 0, 0)),
        ] + [full(p) for p in params],
        out_specs=[
            pl.BlockSpec((bb, t, B_WIDTH), lambda i, j: (i, j, 0)),
            pl.BlockSpec((bb, B_HEADS, B_DK, B_DV), lambda i, j: (i, 0, 0, 0)),
        ],
        out_shape=[
            jax.ShapeDtypeStruct((b, s, B_WIDTH), o_dtype),
            jax.ShapeDtypeStruct((b, B_HEADS, B_DK, B_DV), F32),
        ],
        scratch_shapes=[
            pltpu.VMEM((bb, t + HALO, B_CONV_CH), F32),
            pltpu.VMEM((bb, t, B_CONV_CH), F32),
            pltpu.VMEM((bb, t, B_WIDTH), F32),
            pltpu.VMEM((bb, t, B_WIDTH), F32),
            pltpu.VMEM((bb, B_HEADS, B_DK, B_DV), F32),
            pltpu.VMEM((bb * B_HEADS, t, t), BF16),
            pltpu.VMEM((bb * B_HEADS, t, t), BF16),
            pltpu.VMEM((bb * B_HEADS, t, t), BF16),
            pltpu.VMEM((bb * B_HEADS, t, t), BF16),
        ],
        compiler_params=_cparams(("parallel", "arbitrary")),
        name="gdn_chunk",
    )(qkv, ab, abt, conv0, st0, *params)


def _pad_heads(wm, n_in):
    d = wm.shape[-1]
    return jnp.pad(wm, ((0, 0), (0, 0), (0, SLOT - d))).reshape(n_in, A_HEADS * SLOT)


def _mla_weights(a_norm, a_w_in, a_g_qa, a_w_uq, a_g_kv, a_w_uk, a_w_uv, a_g_q, a_g_k, a_w_o):
    o1 = A_Q_LORA
    o2 = o1 + A_KV_LORA
    o3 = o2 + A_ROPE
    kslot = jnp.pad(a_w_in[:, o2:o3], ((0, 0), (A_NOPE, SLOT - A_QK)))
    w_in = jnp.concatenate([a_w_in[:, :o2], a_w_in[:, o3:], kslot], 1).astype(BF16)
    pad_gain = lambda g: jnp.pad(g, (0, SLOT - A_QK)).reshape(1, SLOT)
    return {
        "a_norm": a_norm.reshape(1, D_MODEL),
        "w_in": w_in,
        "g_qa": a_g_qa.reshape(1, A_Q_LORA),
        "w_uq": _pad_heads(a_w_uq.reshape(A_Q_LORA, A_HEADS, A_QK), A_Q_LORA).astype(BF16),
        "g_kv": a_g_kv.reshape(1, A_KV_LORA),
        "g_q": pad_gain(a_g_q * (A_QK ** -0.5)),
        "g_k": pad_gain(a_g_k),
        "w_uk": _pad_heads(a_w_uk, A_KV_LORA).astype(BF16),
        "w_ukt": jnp.transpose(a_w_uk, (1, 2, 0)).astype(BF16),
        "w_uv": a_w_uv.reshape(A_KV_LORA, A_WIDTH).astype(BF16),
        "w_uv_h": jnp.transpose(a_w_uv, (1, 0, 2)).astype(BF16),
        "w_o": a_w_o.astype(BF16),
    }


def _gdn_weights(b_norm, b_w_in, b_w_conv, b_a_log, b_dt_bias, b_g_o, b_w_o):
    o2 = B_CONV_CH + B_WIDTH
    w_in = jnp.pad(b_w_in, ((0, 0), (0, SLOT - 2 * B_HEADS))).astype(BF16)
    return {
        "b_norm": b_norm.reshape(1, D_MODEL),
        "w_in": w_in,
        "w_abt": b_w_in[:, o2:].T.astype(BF16),
        "w_conv": b_w_conv,
        "a_log_r": b_a_log.reshape(1, B_HEADS),
        "dt_r": b_dt_bias.reshape(1, B_HEADS),
        "a_log_c": b_a_log.reshape(B_HEADS, 1),
        "dt_c": b_dt_bias.reshape(B_HEADS, 1),
        "g_o": b_g_o.reshape(1, B_DV),
        "w_o": b_w_o.astype(BF16),
    }


def _mla_layer(xp, xs, cache_latent, cache_krope, la, page_table, w, tabs_p, tabs_s):
    b, s, _ = xp.shape
    n, t, _ = xs.shape
    q, c_p, kpe_p, z, k, v = _mla_project(xp, tabs_p, w, with_kv=True, tm=256, q_dtype=BF16)
    bound = A_QK * BF16_NORM_SLACK * jnp.max(jnp.abs(w["g_q"])) * jnp.max(jnp.abs(w["g_k"]))
    o = _flash_attention(q, k, v, bound, tq=512, tk=512)
    yp = _gated_out(xp.reshape(b * s, D_MODEL), o.reshape(b * s, A_WIDTH), z.reshape(b * s, A_WIDTH), w["w_o"], tm=512)

    q, c_s, kpe_s, z = _mla_project(xs.reshape(1, n * t, D_MODEL), tabs_s, w, with_kv=False, tm=256, q_dtype=F32)
    c_s = c_s.reshape(n, t, A_KV_LORA)
    kpe_s = kpe_s.reshape(n, t, A_ROPE)
    q8 = jnp.pad(q.reshape(n, t, A_QPAD), ((0, 0), (0, 8 - t), (0, 0)))
    c_new = jnp.pad(c_s, ((0, 0), (0, PAGE - t), (0, 0)))
    kpe_new_t = jnp.pad(jnp.swapaxes(kpe_s, 1, 2), ((0, 0), (0, 0), (0, PAGE - t)))
    o = _paged_attention(q8, c_new, kpe_new_t, cache_latent, jnp.swapaxes(cache_krope, 2, 3), la, page_table, w,
                         ppb=16)
    ys = _gated_out(xs.reshape(n * t, D_MODEL), o[:, :t].reshape(n * t, A_WIDTH), z.reshape(n * t, A_WIDTH),
                    w["w_o"], tm=512)
    return yp.reshape(b, s, D_MODEL), ys.reshape(n, t, D_MODEL), c_p, kpe_p, c_s, kpe_s


def _gdn_layer(xp, xs, state_conv, state_ssm, w):
    b, s, _ = xp.shape
    n, t, _ = xs.shape
    tc = min(256, s)
    qkv, z, ab, abt = _gdn_project(xp.reshape(b * s, D_MODEL), w, tm=256, z_dtype=BF16)
    qkv = qkv.reshape(b, s, B_CONV_CH)
    abt = abt.reshape(2 * B_HEADS, b, s // tc, tc).transpose(1, 2, 0, 3)
    o, st_p = _gdn_chunked(qkv, ab.reshape(b, s, 2 * B_HEADS), abt,
                           jnp.zeros((b, HALO, B_CONV_CH), F32), jnp.zeros((b, B_HEADS, B_DK, B_DV), F32),
                           w, t=tc, valid=tc, bb=1, o_dtype=BF16)
    yp = _gated_out(xp.reshape(b * s, D_MODEL), o.reshape(b * s, B_WIDTH), z, w["w_o"], tm=512)
    conv_p = qkv[:, s - (B_CONV - 1):]

    qkv, z, ab, abt = _gdn_project(xs.reshape(n * t, D_MODEL), w, tm=256, z_dtype=F32)
    qkv = qkv.reshape(n, t, B_CONV_CH)
    pad_t = lambda a: jnp.pad(a, ((0, 0), (0, 8 - t), (0, 0)))
    abt = jnp.pad(abt.reshape(2 * B_HEADS, n, t).transpose(1, 0, 2), ((0, 0), (0, 0), (0, 8 - t)))
    conv0 = jnp.pad(state_conv, ((0, 0), (HALO - (B_CONV - 1), 0), (0, 0)))
    o, st_s = _gdn_chunked(pad_t(qkv), pad_t(ab.reshape(n, t, 2 * B_HEADS)), abt.reshape(n, 1, 2 * B_HEADS, 8),
                           conv0, state_ssm, w, t=8, valid=t, bb=4, o_dtype=F32)
    ys = _gated_out(xs.reshape(n * t, D_MODEL), o[:, :t].reshape(n * t, B_WIDTH), z, w["w_o"], tm=512)
    conv_s = jnp.concatenate([state_conv, qkv], 1)[:, -(B_CONV - 1):]
    return yp.reshape(b, s, D_MODEL), ys.reshape(n, t, D_MODEL), conv_p, st_p, conv_s, st_s


def kernel(x_prompt, x_sample, cache_latent, cache_krope, page_table, state_conv, state_ssm,
           a_norm, a_w_in, a_g_qa, a_w_uq, a_g_kv, a_w_uk, a_w_uv, a_g_q, a_g_k, a_w_o,
           b_norm, b_w_in, b_w_conv, b_a_log, b_dt_bias, b_g_o, b_w_o):
    s = x_prompt.shape[1]
    n, t, _ = x_sample.shape
    past = page_table.shape[1] * PAGE
    p_pad = -(-(s + t) // SLOT) * SLOT
    pos = jnp.concatenate([jnp.arange(s), past + jnp.arange(t), jnp.zeros((p_pad - s - t,), jnp.int32)]).astype(F32)
    tabs = _rope_tables(pos)
    tabs_p = tuple(tb[:s] for tb in tabs)
    tabs_s = tuple(jnp.tile(tb[s:s + t], (n, 1)) for tb in tabs)

    wa = _mla_weights(a_norm[0], a_w_in[0], a_g_qa[0], a_w_uq[0], a_g_kv[0], a_w_uk[0], a_w_uv[0], a_g_q[0],
                      a_g_k[0], a_w_o[0])
    wb = _gdn_weights(b_norm[0], b_w_in[0], b_w_conv[0], b_a_log[0], b_dt_bias[0], b_g_o[0], b_w_o[0])

    xp, xs, lat_p, kpe_p, lat_s, kpe_s = _mla_layer(x_prompt, x_sample, cache_latent, cache_krope, 0, page_table,
                                                    wa, tabs_p, tabs_s)
    xp, xs, conv_p, ssm_p, conv_s, ssm_s = _gdn_layer(xp, xs, state_conv[0], state_ssm[0], wb)
    return (xp, xs, lat_p[None], kpe_p[None], lat_s[None], kpe_s[None],
            conv_p[None], ssm_p[None], conv_s[None], ssm_s[None])
```

```python
import functools

import jax
import jax.numpy as jnp
from jax import lax
from jax.experimental import pallas as pl
from jax.experimental.pallas import tpu as pltpu

F32 = jnp.float32
BF16 = jnp.bfloat16
EPS = 1e-6

D_MODEL = 1024
PAGE = 128
A_HEADS = 8
A_NOPE = 64
A_ROPE = 32
A_QK = A_NOPE + A_ROPE
A_V = 64
A_Q_LORA = 384
A_KV_LORA = 256
A_WIDTH = A_HEADS * A_V
ROPE_THETA = 10000.0
SLOT = 128
A_QPAD = A_HEADS * SLOT
B_HEADS = 8
B_DK = 64
B_DV = 64
B_WIDTH = B_HEADS * B_DV
B_CONV = 4
B_CONV_CH = 2 * B_HEADS * B_DK + B_WIDTH
HALO = 8
BF16_ROWS = 16

VMEM_LIMIT = 56 * 1024 * 1024
FIXED_SHIFT_MAX = 30.0
BF16_NORM_SLACK = 1.02
PAGED_SLOTS = 3
PAGED_PARTS = 2


def _cparams(sem):
    return pltpu.CompilerParams(dimension_semantics=sem, vmem_limit_bytes=VMEM_LIMIT)


def _dot(a, b):
    return jnp.dot(a, b, preferred_element_type=F32)


def _dot_nt(a, b):
    return lax.dot_general(a, b, (((1,), (1,)), ((), ())), preferred_element_type=F32)


def _dot_tn(a, b):
    return lax.dot_general(a, b, (((0,), (0,)), ((), ())), preferred_element_type=F32)


def _dot_sel(sel_bf16, x, *, sel_first):
    x1 = x.astype(BF16)
    r1 = x - x1.astype(F32)
    x2 = r1.astype(BF16)
    x3 = (r1 - x2.astype(F32)).astype(BF16)
    if sel_first:
        return _dot(sel_bf16, x1) + _dot(sel_bf16, x2) + _dot(sel_bf16, x3)
    return _dot(x1, sel_bf16) + _dot(x2, sel_bf16) + _dot(x3, sel_bf16)


def _rms(x, g):
    return x * lax.rsqrt(jnp.mean(x * x, -1, keepdims=True) + EPS) * g


def _silu(x):
    return x * jax.nn.sigmoid(x)


def _softplus(x):
    return jnp.maximum(x, 0.0) + jnp.log(1.0 + jnp.exp(-jnp.abs(x)))


def _rope_table_kernel(pos_ref, inv_ref, cos_ref, sin_ref):
    ang = inv_ref[...] * pos_ref[...]
    cos_ref[...] = jnp.cos(ang)
    sin_ref[...] = jnp.sin(ang)


def _rope_tables(pos):
    half = A_ROPE // 2
    p = pos.shape[0]
    inv = (ROPE_THETA ** (-jnp.arange(half, dtype=F32) / half)).reshape(half, 1)
    cos_t, sin_t = pl.pallas_call(
        _rope_table_kernel,
        out_shape=(jax.ShapeDtypeStruct((half, p), F32),) * 2,
        name="rope_tables",
    )(pos.reshape(1, p), inv)
    cos = cos_t.T
    sin = sin_t.T
    one = jnp.ones((p, A_NOPE), F32)
    zn = jnp.zeros((p, A_NOPE), F32)
    zh = jnp.zeros((p, half), F32)
    zp = jnp.zeros((p, SLOT - A_QK), F32)
    tc = jnp.concatenate([one, cos, cos, zp], 1)
    ts1 = jnp.concatenate([zn, -sin, zh, zp], 1)
    ts2 = jnp.concatenate([zn, zh, sin, zp], 1)
    return tc, ts1, ts2


def _mla_proj_kernel(*refs, with_kv):
    (x_ref, tc_ref, ts1_ref, ts2_ref, gn_ref, win_ref, gqa_ref, wuq_ref, gkv_ref, gq_ref, ones_ref) = refs[:11]
    if with_kv:
        wuk_ref, gk_ref, wuv_ref = refs[11:14]
        q_out, c_out, kpe_out, z_out, k_out, v_out = refs[14:]
    else:
        q_out, c_out, kpe_out, z_out = refs[11:]
    x = x_ref[0]
    h = _rms(x, gn_ref[...])
    proj = _dot(h.astype(BF16), win_ref[...])
    o1 = A_Q_LORA
    o2 = o1 + A_KV_LORA
    o3 = o2 + A_WIDTH
    qa = _rms(proj[:, :o1], gqa_ref[...])
    c = _rms(proj[:, o1:o2], gkv_ref[...])
    z_out[0] = proj[:, o2:o3].astype(z_out.dtype)
    c_out[0] = c
    q = _dot(qa.astype(BF16), wuq_ref[...])
    tc = tc_ref[...]
    ts1 = ts1_ref[...]
    ts2 = ts2_ref[...]

    def rope(s):
        return s * tc + pltpu.roll(s, SLOT - A_ROPE // 2, 1) * ts1 + pltpu.roll(s, A_ROPE // 2, 1) * ts2

    ones = ones_ref[...]

    def head_norm(s, g):
        s2 = s * s
        s2h = s2.astype(BF16)
        ms = (_dot(s2h, ones) + _dot((s2 - s2h.astype(F32)).astype(BF16), ones)) * (1.0 / A_QK)
        return s * lax.rsqrt(ms + EPS) * g

    kslot = rope(proj[:, o3:o3 + SLOT])
    kpe_out[0] = kslot[:, A_NOPE:A_QK]
    gq = gq_ref[...]
    for hd in range(A_HEADS):
        sl = slice(hd * SLOT, (hd + 1) * SLOT)
        q_out[0, :, sl] = head_norm(rope(q[:, sl]), gq).astype(q_out.dtype)
    if with_kv:
        cb = c.astype(BF16)
        kn = _dot(cb, wuk_ref[...])
        v_out[0] = _dot(cb, wuv_ref[...]).astype(v_out.dtype)
        gk = gk_ref[...]
        for hd in range(A_HEADS):
            sl = slice(hd * SLOT, (hd + 1) * SLOT)
            k_out[0, :, sl] = head_norm(kn[:, sl] + kslot, gk).astype(k_out.dtype)


def _mla_project(x, tabs, w, *, with_kv, tm, q_dtype):
    b, s, _ = x.shape
    tm = min(tm, s)
    grid = (s // tm, b)
    full = lambda a: pl.BlockSpec(a.shape, lambda i, j: (0,) * a.ndim)
    row = lambda n: pl.BlockSpec((1, tm, n), lambda i, j: (j, i, 0))
    tab = pl.BlockSpec((tm, SLOT), lambda i, j: (i, 0))
    params = [w["a_norm"], w["w_in"], w["g_qa"], w["w_uq"], w["g_kv"], w["g_q"], jnp.ones((SLOT, SLOT), BF16)]
    out_shape = [
        jax.ShapeDtypeStruct((b, s, A_QPAD), q_dtype),
        jax.ShapeDtypeStruct((b, s, A_KV_LORA), F32),
        jax.ShapeDtypeStruct((b, s, A_ROPE), F32),
        jax.ShapeDtypeStruct((b, s, A_WIDTH), q_dtype),
    ]
    out_specs = [row(A_QPAD), row(A_KV_LORA), row(A_ROPE), row(A_WIDTH)]
    if with_kv:
        params += [w["w_uk"], w["g_k"], w["w_uv"]]
        out_shape += [jax.ShapeDtypeStruct((b, s, A_QPAD), BF16), jax.ShapeDtypeStruct((b, s, A_WIDTH), BF16)]
        out_specs += [row(A_QPAD), row(A_WIDTH)]
    return pl.pallas_call(
        functools.partial(_mla_proj_kernel, with_kv=with_kv),
        grid=grid,
        in_specs=[row(D_MODEL), tab, tab, tab] + [full(p) for p in params],
        out_specs=out_specs,
        out_shape=out_shape,
        compiler_params=_cparams(("parallel", "parallel")),
        name="mla_proj_kv" if with_kv else "mla_proj",
    )(x, *tabs, *params)


def _flash_kernel(qi_ref, ki_ref, dv_ref, fx_ref, q_ref, k_ref, v_ref, sh_ref, o_ref, m_ref, l_ref, acc_ref,
                  *, tq, tk):
    del qi_ref
    p = pl.program_id(1)
    ki = ki_ref[p]
    dv = dv_ref[p]
    fixed = fx_ref[0] == 1

    @pl.when(ki == 0)
    def _():
        m_ref[...] = jnp.full(m_ref.shape, -jnp.inf, F32)
        l_ref[...] = jnp.zeros(l_ref.shape, F32)
        acc_ref[...] = jnp.zeros(acc_ref.shape, F32)

    low = lax.broadcasted_iota(jnp.int32, (tq, SLOT), 1) < A_V

    def step(d, fixed_shift):
        diag = d is not None
        if diag:
            keep = (lax.broadcasted_iota(jnp.int32, (tq, tk), 1) + d * tk
                    <= lax.broadcasted_iota(jnp.int32, (tq, tk), 0))
        if fixed_shift:
            shift = sh_ref[:, :1]
        for j in range(A_HEADS // 2):
            pv = []
            al = []
            vpair = v_ref[0, :, j * SLOT:(j + 1) * SLOT]
            for e in range(2):
                hd = 2 * j + e
                sl = slice(hd * SLOT, (hd + 1) * SLOT)
                s = _dot_nt(q_ref[0, :, sl], k_ref[0, :, sl])
                if diag:
                    s = jnp.where(keep, s, -jnp.inf)
                if fixed_shift:
                    pr = jnp.exp(s - shift)
                    l_ref[hd] = l_ref[hd] + jnp.sum(pr, -1, keepdims=True)
                else:
                    m_prev = m_ref[hd]
                    m_new = jnp.maximum(m_prev, jnp.max(s, -1, keepdims=True))
                    alpha = jnp.exp(m_prev - m_new)
                    pr = jnp.exp(s - m_new[:, :1])
                    l_ref[hd] = alpha * l_ref[hd] + jnp.sum(pr, -1, keepdims=True)
                    m_ref[hd] = m_new
                    al.append(alpha)
                pv.append(_dot(pr.astype(BF16), vpair))
            sl = slice(j * SLOT, (j + 1) * SLOT)
            if fixed_shift:
                acc_ref[:, sl] = acc_ref[:, sl] + jnp.where(low, pv[0], pv[1])
            else:
                acc_ref[:, sl] = jnp.where(low, al[0], al[1]) * acc_ref[:, sl] + jnp.where(low, pv[0], pv[1])

    def finish():
        for j in range(A_HEADS // 2):
            sl = slice(j * SLOT, (j + 1) * SLOT)
            linv = jnp.where(low, 1.0 / l_ref[2 * j], 1.0 / l_ref[2 * j + 1])
            o_ref[0, :, sl] = (acc_ref[:, sl] * linv).astype(o_ref.dtype)

    n_diag = tq // tk
    for fs in (True, False):
        mode = fixed if fs else jnp.logical_not(fixed)

        @pl.when(jnp.logical_and(mode, dv < 0))
        def _():
            step(None, fs)

        for d in range(n_diag):
            @pl.when(jnp.logical_and(mode, dv == d))
            def _():
                step(d, fs)
                if d == n_diag - 1:
                    finish()


def _flash_attention(q, k, v, bound, *, tq, tk):
    b, s, _ = q.shape
    tq = min(tq, s)
    tk = min(tk, tq)
    assert s % tq == 0 and tq % tk == 0
    nq = s // tq
    r = tq // tk
    pairs = [(i, j, j - r * i if j >= r * i else -1) for i in range(nq) for j in range(r * (i + 1))]
    qi = jnp.asarray([pr[0] for pr in pairs], jnp.int32)
    ki = jnp.asarray([pr[1] for pr in pairs], jnp.int32)
    dv = jnp.asarray([pr[2] for pr in pairs], jnp.int32)
    fx = (bound <= FIXED_SHIFT_MAX).astype(jnp.int32).reshape(1)
    sh = jnp.full((1, SLOT), bound, F32)
    grid_spec = pltpu.PrefetchScalarGridSpec(
        num_scalar_prefetch=4,
        grid=(b, len(pairs)),
        in_specs=[
            pl.BlockSpec((1, tq, A_QPAD), lambda bi, p, qt, kt, dt, fx: (bi, qt[p], 0)),
            pl.BlockSpec((1, tk, A_QPAD), lambda bi, p, qt, kt, dt, fx: (bi, kt[p], 0)),
            pl.BlockSpec((1, tk, A_WIDTH), lambda bi, p, qt, kt, dt, fx: (bi, kt[p], 0)),
            pl.BlockSpec((1, SLOT), lambda bi, p, qt, kt, dt, fx: (0, 0)),
        ],
        out_specs=pl.BlockSpec((1, tq, A_WIDTH), lambda bi, p, qt, kt, dt, fx: (bi, qt[p], 0)),
        scratch_shapes=[
            pltpu.VMEM((A_HEADS, tq, SLOT), F32),
            pltpu.VMEM((A_HEADS, tq, SLOT), F32),
            pltpu.VMEM((tq, A_WIDTH), F32),
        ],
    )
    return pl.pallas_call(
        functools.partial(_flash_kernel, tq=tq, tk=tk),
        grid_spec=grid_spec,
        out_shape=jax.ShapeDtypeStruct((b, s, A_WIDTH), BF16),
        compiler_params=_cparams(("parallel", "arbitrary")),
        name="mla_flash",
    )(qi, ki, dv, fx, q, k, v, sh)


def _paged_kernel(pt_ref, q_ref, cn_ref, kn_ref, gk_ref, wukt_ref, wuv_ref, lat_hbm, kr_hbm, o_ref,
                  wq_ref, qr_ref, m_ref, l_ref, acc_ref, lat_buf, kr_buf, sem, *, la, ppb, n_blocks):
    smp = pl.program_id(0)
    blk = pl.program_id(1)
    rows = A_HEADS * 8
    hn = A_HEADS * A_NOPE

    step = smp * n_blocks + blk
    n_steps = pl.num_programs(0) * n_blocks
    ahead = PAGED_SLOTS - 1

    def page_copies(g):
        s_ = g // n_blocks
        b_ = g % n_blocks
        slot_ = g % PAGED_SLOTS
        out = []
        for i in range(ppb):
            page = pt_ref[s_, b_ * ppb + i]
            out.append(pltpu.make_async_copy(lat_hbm.at[la, page], lat_buf.at[slot_, pl.ds(i * PAGE, PAGE)],
                                             sem.at[slot_]))
            out.append(pltpu.make_async_copy(kr_hbm.at[la, page], kr_buf.at[slot_, :, pl.ds(i * PAGE, PAGE)],
                                             sem.at[slot_]))
        return out

    for d in range(ahead):
        @pl.when(jnp.logical_and(step == 0, d < n_steps))
        def _():
            for cp in page_copies(step + d):
                cp.start()

    @pl.when(step + ahead < n_steps)
    def _():
        for cp in page_copies(step + ahead):
            cp.start()

    for cp in page_copies(step):
        cp.wait()
    slot = step % PAGED_SLOTS

    @pl.when(blk == 0)
    def _():
        m_ref[...] = jnp.full(m_ref.shape, -jnp.inf, F32)
        l_ref[...] = jnp.zeros(l_ref.shape, F32)
        acc_ref[...] = jnp.zeros(acc_ref.shape, F32)
        gk = gk_ref[...]
        for hd in range(A_HEADS):
            qs = q_ref[0, :, hd * SLOT:(hd + 1) * SLOT] * gk
            wq_ref[hd * A_NOPE:(hd + 1) * A_NOPE, :] = wukt_ref[hd]
            wq_ref[hn + hd * 8:hn + (hd + 1) * 8, :] = _dot(qs[:, :A_NOPE].astype(BF16), wukt_ref[hd]).astype(BF16)
            qr_ref[hd * 8:(hd + 1) * 8, :] = qs[:, A_NOPE:A_QK].astype(BF16)

    def scores(cb, krt):
        nk = cb.shape[0]
        res = _dot_nt(wq_ref[...], cb)
        ssq = []
        for hd in range(A_HEADS):
            kh = res[hd * A_NOPE:(hd + 1) * A_NOPE]
            ssq.append(jnp.broadcast_to(jnp.sum(kh * kh, 0, keepdims=True), (8, nk)))
        ssq = jnp.concatenate(ssq, 0) + jnp.sum(krt * krt, 0, keepdims=True)
        s = res[hn:] + _dot(qr_ref[...], krt.astype(BF16))
        return s * lax.rsqrt(ssq * (1.0 / A_QK) + EPS)

    def block(cbs, krts, new):
        sts = [scores(cb, krt) for cb, krt in zip(cbs, krts)]
        if new:
            nk = cbs[0].shape[0]
            tok = lax.broadcasted_iota(jnp.int32, (rows, nk), 0) & 7
            key = lax.broadcasted_iota(jnp.int32, (rows, nk), 1)
            sts = [jnp.where(key <= tok, st, -jnp.inf) for st in sts]
        ms = [jnp.max(st, -1, keepdims=True) for st in sts]
        prs = [jnp.exp(st - m) for st, m in zip(sts, ms)]
        ls = [jnp.sum(pr, -1, keepdims=True) for pr in prs]
        pvs = [_dot(pr.astype(BF16), cb) for pr, cb in zip(prs, cbs)]
        m_prev = m_ref[...]
        m_new = m_prev
        for m in ms:
            m_new = jnp.maximum(m_new, m)
        alpha = jnp.exp(m_prev - m_new)
        l_new = alpha * l_ref[...]
        acc = jnp.concatenate([alpha, alpha], 1) * acc_ref[...]
        for m, l, pv in zip(ms, ls, pvs):
            wgt = jnp.exp(m - m_new)
            l_new = l_new + wgt * l
            acc = acc + jnp.concatenate([wgt, wgt], 1) * pv
        m_ref[...] = m_new
        l_ref[...] = l_new
        acc_ref[...] = acc

    sub = max(ppb // PAGED_PARTS, 1) * PAGE
    block([lat_buf[slot, i:i + sub].astype(BF16) for i in range(0, ppb * PAGE, sub)],
          [kr_buf[slot, :, i:i + sub] for i in range(0, ppb * PAGE, sub)], False)

    @pl.when(blk == n_blocks - 1)
    def _():
        block([cn_ref[0].astype(BF16)], [kn_ref[0]], True)
        o_lat = (acc_ref[...] * (1.0 / l_ref[...][:, :1])).astype(BF16)
        for hd in range(A_HEADS):
            o_ref[0, :, hd * A_V:(hd + 1) * A_V] = _dot(o_lat[hd * 8:(hd + 1) * 8], wuv_ref[hd])


def _paged_attention(q, c_new, kpe_new_t, cache_latent, cache_krope_t, la, page_table, w, *, ppb):
    n = q.shape[0]
    n_pages = page_table.shape[1]
    ppb = min(ppb, n_pages)
    n_blocks = n_pages // ppb
    full = lambda a: pl.BlockSpec(a.shape, lambda i, j, pt: (0,) * a.ndim)
    rows = A_HEADS * 8
    params = [w["g_k"], w["w_ukt"], w["w_uv_h"]]

    hbm = pl.BlockSpec(memory_space=pl.ANY)
    grid_spec = pltpu.PrefetchScalarGridSpec(
        num_scalar_prefetch=1,
        grid=(n, n_blocks),
        in_specs=[
            pl.BlockSpec((1, 8, A_QPAD), lambda s, j, pt: (s, 0, 0)),
            pl.BlockSpec((1, PAGE, A_KV_LORA), lambda s, j, pt: (s, 0, 0)),
            pl.BlockSpec((1, A_ROPE, PAGE), lambda s, j, pt: (s, 0, 0)),
        ] + [full(p) for p in params] + [hbm, hbm],
        out_specs=pl.BlockSpec((1, 8, A_WIDTH), lambda s, j, pt: (s, 0, 0)),
        scratch_shapes=[
            pltpu.VMEM((A_HEADS * A_NOPE + rows, A_KV_LORA), BF16),
            pltpu.VMEM((rows, A_ROPE), BF16),
            pltpu.VMEM((rows, SLOT), F32),
            pltpu.VMEM((rows, SLOT), F32),
            pltpu.VMEM((rows, A_KV_LORA), F32),
            pltpu.VMEM((PAGED_SLOTS, ppb * PAGE, A_KV_LORA), F32),
            pltpu.VMEM((PAGED_SLOTS, A_ROPE, ppb * PAGE), F32),
            pltpu.SemaphoreType.DMA((PAGED_SLOTS,)),
        ],
    )
    return pl.pallas_call(
        functools.partial(_paged_kernel, la=la, ppb=ppb, n_blocks=n_blocks),
        grid_spec=grid_spec,
        out_shape=jax.ShapeDtypeStruct((n, 8, A_WIDTH), F32),
        compiler_params=_cparams(("arbitrary", "arbitrary")),
        name="mla_paged",
    )(page_table, q, c_new, kpe_new_t, *params, cache_latent, cache_krope_t)


def _gated_out_kernel(x_ref, o_ref, z_ref, w_ref, y_ref):
    gated = o_ref[...].astype(F32) * _silu(z_ref[...].astype(F32))
    y_ref[...] = x_ref[...] + _dot(gated.astype(BF16), w_ref[...])


def _gated_out(x, o, z, w_o, *, tm):
    r = x.shape[0]
    tm = min(tm, r)
    row = lambda n: pl.BlockSpec((tm, n), lambda i: (i, 0))
    return pl.pallas_call(
        _gated_out_kernel,
        grid=(r // tm,),
        in_specs=[row(D_MODEL), row(o.shape[1]), row(z.shape[1]), pl.BlockSpec(w_o.shape, lambda i: (0, 0))],
        out_specs=row(D_MODEL),
        out_shape=jax.ShapeDtypeStruct((r, D_MODEL), F32),
        compiler_params=_cparams(("parallel",)),
        name="gated_out",
    )(x, o, z, w_o)


def _gdn_proj_kernel(x_ref, gn_ref, w_ref, wabt_ref, qkv_out, z_out, ab_out, abt_out):
    hb = _rms(x_ref[...], gn_ref[...]).astype(BF16)
    proj = _dot(hb, w_ref[...])
    o1 = B_CONV_CH
    o2 = o1 + B_WIDTH
    qkv_out[...] = proj[:, :o1]
    z_out[...] = proj[:, o1:o2].astype(z_out.dtype)
    ab_out[...] = proj[:, o2:o2 + 2 * B_HEADS]
    abt_out[...] = _dot_nt(wabt_ref[...], hb)


def _gdn_project(x, w, *, tm, z_dtype):
    r = x.shape[0]
    tm = min(tm, r)
    row = lambda n: pl.BlockSpec((tm, n), lambda i: (i, 0))
    full = lambda a: pl.BlockSpec(a.shape, lambda i: (0,) * a.ndim)
    params = [w["b_norm"], w["w_in"], w["w_abt"]]
    return pl.pallas_call(
        _gdn_proj_kernel,
        grid=(r // tm,),
        in_specs=[row(D_MODEL)] + [full(p) for p in params],
        out_specs=[row(B_CONV_CH), row(B_WIDTH), row(2 * B_HEADS), pl.BlockSpec((2 * B_HEADS, tm), lambda i: (0, i))],
        out_shape=[
            jax.ShapeDtypeStruct((r, B_CONV_CH), F32),
            jax.ShapeDtypeStruct((r, B_WIDTH), z_dtype),
            jax.ShapeDtypeStruct((r, 2 * B_HEADS), F32),
            jax.ShapeDtypeStruct((2 * B_HEADS, r), F32),
        ],
        compiler_params=_cparams(("parallel",)),
        name="gdn_proj",
    )(x, *params)


def _gdn_chunk_kernel(qkv_ref, ab_ref, abt_ref, conv0_ref, st0_ref, wc_ref, alr_ref, dtr_ref, alc_ref, dtc_ref,
                      go_ref, exp_ref, tri_ref, triu_ref, off_ref, avg_ref, o_ref, st_out, ext_ref, feat_ref, gcx_ref,
                      bx_ref, st_ref, lm_ref, inv_ref, t1_ref, in_ref, *, t, valid, bb):
    si = pl.program_id(1)
    hk = B_HEADS * B_DK
    units = [(bi, hd) for bi in range(bb) for hd in range(B_HEADS)]
    qsl = lambda hd: slice(hd * B_DK, (hd + 1) * B_DK)
    ksl = lambda hd: slice(hk + hd * B_DK, hk + (hd + 1) * B_DK)
    vsl = lambda hd: slice(2 * hk + hd * B_DV, 2 * hk + (hd + 1) * B_DV)

    @pl.when(si == 0)
    def _():
        ext_ref[:, 0:HALO] = conv0_ref[...]
        st_ref[...] = st0_ref[...]

    gcts = []
    for bi in range(bb):
        ext_ref[bi, HALO:HALO + t] = qkv_ref[bi]
        conv = ext_ref[bi, HALO - 3:HALO - 3 + t] * wc_ref[0:1]
        for j in range(1, B_CONV):
            conv = conv + ext_ref[bi, HALO - 3 + j:HALO - 3 + j + t] * wc_ref[j:j + 1]
        feat_ref[bi] = _silu(conv)
        ext_ref[bi, 0:HALO] = ext_ref[bi, t:t + HALO]

        ab = ab_ref[bi]
        g = -jnp.exp(alr_ref[...]) * _softplus(ab[:, :B_HEADS] + dtr_ref[...])
        beta = jax.nn.sigmoid(ab[:, B_HEADS:])
        if valid < t:
            live = lax.broadcasted_iota(jnp.int32, (t, B_HEADS), 0) < valid
            g = jnp.where(live, g, 0.0)
            beta = jnp.where(live, beta, 0.0)
        bx_ref[bi] = _dot_sel(exp_ref[...], beta, sel_first=False)
        gcx_ref[bi] = _dot_sel(tri_ref[...], _dot_sel(exp_ref[...], g, sel_first=False), sel_first=True)
        gt = -jnp.exp(alc_ref[...]) * _softplus(abt_ref[bi, 0][:B_HEADS] + dtc_ref[...])
        if valid < t:
            gt = jnp.where(lax.broadcasted_iota(jnp.int32, (B_HEADS, t), 1) < valid, gt, 0.0)
        gcts.append(_dot_sel(triu_ref[...], gt, sel_first=False))

    ii = lax.broadcasted_iota(jnp.int32, (t, t), 0)
    jj = lax.broadcasted_iota(jnp.int32, (t, t), 1)
    incl = ii >= jj
    strict = ii > jj
    eye = (ii == jj).astype(F32)
    pair = (ii >> 1) == (jj >> 1)

    def cols(x):
        return x[:, :t] if t <= B_DK else jnp.tile(x, (1, t // B_DK))

    for u, (bi, hd) in enumerate(units):
        qh = feat_ref[bi, :, qsl(hd)]
        kh = feat_ref[bi, :, ksl(hd)]
        qh = qh * lax.rsqrt(jnp.sum(qh * qh, -1, keepdims=True) + EPS) * (B_DK ** -0.5)
        kh = kh * lax.rsqrt(jnp.sum(kh * kh, -1, keepdims=True) + EPS)
        feat_ref[bi, :, qsl(hd)] = qh
        feat_ref[bi, :, ksl(hd)] = kh
        decay = jnp.exp(jnp.where(incl, cols(gcx_ref[bi, :, qsl(hd)]) - gcts[bi][hd:hd + 1, :], -jnp.inf))
        khb = kh.astype(BF16)
        lmat = jnp.where(strict, _dot_nt((kh * bx_ref[bi, :, qsl(hd)]).astype(BF16), khb) * decay, 0.0)
        lm_ref[u] = lmat.astype(BF16)
        in_ref[u] = (_dot_nt(qh.astype(BF16), khb) * decay).astype(BF16)
        inv_ref[u] = (eye - jnp.where(pair, lmat, 0.0)).astype(BF16)

    for lv in range(off_ref.shape[0]):
        bsz = 2 << lv
        if bsz % BF16_ROWS == 0:
            rngs = [((2 * m + 1) * bsz, (2 * m + 2) * bsz) for m in range(t // (2 * bsz))]
            rows_of = lambda ref, i: jnp.concatenate([ref[i, a:b, :] for a, b in rngs], 0)
            zero = jnp.zeros((bsz, t), BF16)
            for u in range(len(units)):
                half = _dot(rows_of(lm_ref, u) * rows_of(off_ref, lv), inv_ref[u]).astype(BF16)
                pieces = []
                for m in range(len(rngs)):
                    pieces += [zero, half[m * bsz:(m + 1) * bsz]]
                t1_ref[u] = jnp.concatenate(pieces, 0)
            for u in range(len(units)):
                upd = _dot(rows_of(inv_ref, u), t1_ref[u]).astype(BF16)
                for m, (a, b) in enumerate(rngs):
                    inv_ref[u, a:b, :] = inv_ref[u, a:b, :] - upd[m * bsz:(m + 1) * bsz]
        else:
            for u in range(len(units)):
                t1_ref[u] = _dot(lm_ref[u] * off_ref[lv], inv_ref[u]).astype(BF16)
            for u in range(len(units)):
                inv_ref[u] = inv_ref[u] - _dot(inv_ref[u], t1_ref[u]).astype(BF16)

    go = go_ref[...]
    us, ws, egs = [], [], []
    for u, (bi, hd) in enumerate(units):
        invb = inv_ref[u]
        bc = bx_ref[bi, :, qsl(hd)]
        eg = jnp.exp(gcx_ref[bi, :, qsl(hd)])
        kb = feat_ref[bi, :, ksl(hd)] * bc
        us.append(_dot(invb, (feat_ref[bi, :, vsl(hd)] * bc).astype(BF16)))
        ws.append(_dot(invb, (kb * eg).astype(BF16)).astype(BF16))
        egs.append(eg)
    vns = []
    for u, (bi, hd) in enumerate(units):
        vns.append((us[u] - _dot(ws[u], st_ref[bi, hd].astype(BF16))).astype(BF16))
    for u, (bi, hd) in enumerate(units):
        st = st_ref[bi, hd]
        o = _dot((feat_ref[bi, :, qsl(hd)] * egs[u]).astype(BF16), st.astype(BF16)) + _dot(in_ref[u], vns[u])
        gcc = gcx_ref[bi, :, qsl(hd)]
        glast = gcc[t - 1:t, :]
        kdec = feat_ref[bi, :, ksl(hd)] * jnp.exp(glast - gcc)
        st_ref[bi, hd] = st * jnp.exp(glast) + _dot_tn(kdec.astype(BF16), vns[u])
        if t >= B_DV:
            o2 = o * o
            o2h = o2.astype(BF16)
            ms = _dot(o2h, avg_ref[...]) + _dot((o2 - o2h.astype(F32)).astype(BF16), avg_ref[...])
        else:
            ms = jnp.mean(o * o, -1, keepdims=True)
        o_ref[bi, :, qsl(hd)] = (o * lax.rsqrt(ms + EPS) * go).astype(o_ref.dtype)

    @pl.when(si == pl.num_programs(1) - 1)
    def _():
        st_out[...] = st_ref[...]


def _gdn_chunked(qkv, ab, abt, conv0, st0, w, *, t, valid, bb, o_dtype):
    b, s, _ = qkv.shape
    t = min(t, s)
    full = lambda a: pl.BlockSpec(a.shape, lambda i, j: (0,) * a.ndim)
    expand = jnp.repeat(jnp.eye(B_HEADS, dtype=BF16), B_DV, axis=1)
    pos = jnp.arange(t)
    tri = (pos[:, None] >= pos[None, :]).astype(BF16)
    triu = (pos[:, None] <= pos[None, :]).astype(BF16)
    blk = lambda lg: pos >> lg
    off = jnp.stack([((blk(lg)[:, None] - blk(lg)[None, :] == 1) & ((blk(lg)[:, None] & 1) == 1)).astype(BF16)
                     for lg in range(1, t.bit_length() - 1)])
    avg = jnp.full((B_DV, B_DV), 1.0 / B_DV, BF16)
    params = [w["w_conv"], w["a_log_r"], w["dt_r"], w["a_log_c"], w["dt_c"], w["g_o"], expand, tri, triu, off, avg]
    bb = min(bb, b)
    assert b % bb == 0 and s % t == 0
    return pl.pallas_call(
        functools.partial(_gdn_chunk_kernel, t=t, valid=valid, bb=bb),
        grid=(b // bb, s // t),
        in_specs=[
            pl.BlockSpec((bb, t, B_CONV_CH), lambda i, j: (i, j, 0)),
            pl.BlockSpec((bb, t, 2 * B_HEADS), lambda i, j: (i, j, 0)),
            pl.BlockSpec((bb, 1, 2 * B_HEADS, t), lambda i, j: (i, j, 0, 0)),
            pl.BlockSpec((bb, HALO, B_CONV_CH), lambda i, j: (i, 0, 0)),
            pl.BlockSpec((bb, B_HEADS, B_DK, B_DV), lambda i, j: (i, 0, 0, 0)),
        ] + [full(p) for p in params],
        out_specs=[
            pl.BlockSpec((bb, t, B_WIDTH), lambda i, j: (i, j, 0)),
            pl.BlockSpec((bb, B_HEADS, B_DK, B_DV), lambda i, j: (i, 0, 0, 0)),
        ],
        out_shape=[
            jax.ShapeDtypeStruct((b, s, B_WIDTH), o_dtype),
            jax.ShapeDtypeStruct((b, B_HEADS, B_DK, B_DV), F32),
        ],
        scratch_shapes=[
            pltpu.VMEM((bb, t + HALO, B_CONV_CH), F32),
            pltpu.VMEM((bb, t, B_CONV_CH), F32),
            pltpu.VMEM((bb, t, B_WIDTH), F32),
            pltpu.VMEM((bb, t, B_WIDTH), F32),
            pltpu.VMEM((bb, B_HEADS, B_DK, B_DV), F32),
            pltpu.VMEM((bb * B_HEADS, t, t), BF16),
            pltpu.VMEM((bb * B_HEADS, t, t), BF16),
            pltpu.VMEM((bb * B_HEADS, t, t), BF16),
            pltpu.VMEM((bb * B_HEADS, t, t), BF16),
        ],
        compiler_params=_cparams(("parallel", "arbitrary")),
        name="gdn_chunk",
    )(qkv, ab, abt, conv0, st0, *params)


def _pad_heads(wm, n_in):
    d = wm.shape[-1]
    return jnp.pad(wm, ((0, 0), (0, 0), (0, SLOT - d))).reshape(n_in, A_HEADS * SLOT)


def _mla_weights(a_norm, a_w_in, a_g_qa, a_w_uq, a_g_kv, a_w_uk, a_w_uv, a_g_q, a_g_k, a_w_o):
    o1 = A_Q_LORA
    o2 = o1 + A_KV_LORA
    o3 = o2 + A_ROPE
    kslot = jnp.pad(a_w_in[:, o2:o3], ((0, 0), (A_NOPE, SLOT - A_QK)))
    w_in = jnp.concatenate([a_w_in[:, :o2], a_w_in[:, o3:], kslot], 1).astype(BF16)
    pad_gain = lambda g: jnp.pad(g, (0, SLOT - A_QK)).reshape(1, SLOT)
    return {
        "a_norm": a_norm.reshape(1, D_MODEL),
        "w_in": w_in,
        "g_qa": a_g_qa.reshape(1, A_Q_LORA),
        "w_uq": _pad_heads(a_w_uq.reshape(A_Q_LORA, A_HEADS, A_QK), A_Q_LORA).astype(BF16),
        "g_kv": a_g_kv.reshape(1, A_KV_LORA),
        "g_q": pad_gain(a_g_q * (A_QK ** -0.5)),
        "g_k": pad_gain(a_g_k),
        "w_uk": _pad_heads(a_w_uk, A_KV_LORA).astype(BF16),
        "w_ukt": jnp.transpose(a_w_uk, (1, 2, 0)).astype(BF16),
        "w_uv": a_w_uv.reshape(A_KV_LORA, A_WIDTH).astype(BF16),
        "w_uv_h": jnp.transpose(a_w_uv, (1, 0, 2)).astype(BF16),
        "w_o": a_w_o.astype(BF16),
    }


def _gdn_weights(b_norm, b_w_in, b_w_conv, b_a_log, b_dt_bias, b_g_o, b_w_o):
    o2 = B_CONV_CH + B_WIDTH
    w_in = jnp.pad(b_w_in, ((0, 0), (0, SLOT - 2 * B_HEADS))).astype(BF16)
    return {
        "b_norm": b_norm.reshape(1, D_MODEL),
        "w_in": w_in,
        "w_abt": b_w_in[:, o2:].T.astype(BF16),
        "w_conv": b_w_conv,
        "a_log_r": b_a_log.reshape(1, B_HEADS),
        "dt_r": b_dt_bias.reshape(1, B_HEADS),
        "a_log_c": b_a_log.reshape(B_HEADS, 1),
        "dt_c": b_dt_bias.reshape(B_HEADS, 1),
        "g_o": b_g_o.reshape(1, B_DV),
        "w_o": b_w_o.astype(BF16),
    }


def _mla_layer(xp, xs, cache_latent, cache_krope, la, page_table, w, tabs_p, tabs_s):
    b, s, _ = xp.shape
    n, t, _ = xs.shape
    q, c_p, kpe_p, z, k, v = _mla_project(xp, tabs_p, w, with_kv=True, tm=256, q_dtype=BF16)
    bound = A_QK * BF16_NORM_SLACK * jnp.max(jnp.abs(w["g_q"])) * jnp.max(jnp.abs(w["g_k"]))
    o = _flash_attention(q, k, v, bound, tq=512, tk=512)
    yp = _gated_out(xp.reshape(b * s, D_MODEL), o.reshape(b * s, A_WIDTH), z.reshape(b * s, A_WIDTH), w["w_o"], tm=512)

    q, c_s, kpe_s, z = _mla_project(xs.reshape(1, n * t, D_MODEL), tabs_s, w, with_kv=False, tm=256, q_dtype=F32)
    c_s = c_s.reshape(n, t, A_KV_LORA)
    kpe_s = kpe_s.reshape(n, t, A_ROPE)
    q8 = jnp.pad(q.reshape(n, t, A_QPAD), ((0, 0), (0, 8 - t), (0, 0)))
    c_new = jnp.pad(c_s, ((0, 0), (0, PAGE - t), (0, 0)))
    kpe_new_t = jnp.pad(jnp.swapaxes(kpe_s, 1, 2), ((0, 0), (0, 0), (0, PAGE - t)))
    o = _paged_attention(q8, c_new, kpe_new_t, cache_latent, jnp.swapaxes(cache_krope, 2, 3), la, page_table, w,
                         ppb=32)
    ys = _gated_out(xs.reshape(n * t, D_MODEL), o[:, :t].reshape(n * t, A_WIDTH), z.reshape(n * t, A_WIDTH),
                    w["w_o"], tm=512)
    return yp.reshape(b, s, D_MODEL), ys.reshape(n, t, D_MODEL), c_p, kpe_p, c_s, kpe_s


def _gdn_layer(xp, xs, state_conv, state_ssm, w):
    b, s, _ = xp.shape
    n, t, _ = xs.shape
    tc = min(256, s)
    qkv, z, ab, abt = _gdn_project(xp.reshape(b * s, D_MODEL), w, tm=256, z_dtype=BF16)
    qkv = qkv.reshape(b, s, B_CONV_CH)
    abt = abt.reshape(2 * B_HEADS, b, s // tc, tc).transpose(1, 2, 0, 3)
    o, st_p = _gdn_chunked(qkv, ab.reshape(b, s, 2 * B_HEADS), abt,
                           jnp.zeros((b, HALO, B_CONV_CH), F32), jnp.zeros((b, B_HEADS, B_DK, B_DV), F32),
                           w, t=tc, valid=tc, bb=1, o_dtype=BF16)
    yp = _gated_out(xp.reshape(b * s, D_MODEL), o.reshape(b * s, B_WIDTH), z, w["w_o"], tm=512)
    conv_p = qkv[:, s - (B_CONV - 1):]

    qkv, z, ab, abt = _gdn_project(xs.reshape(n * t, D_MODEL), w, tm=256, z_dtype=F32)
    qkv = qkv.reshape(n, t, B_CONV_CH)
    pad_t = lambda a: jnp.pad(a, ((0, 0), (0, 8 - t), (0, 0)))
    abt = jnp.pad(abt.reshape(2 * B_HEADS, n, t).transpose(1, 0, 2), ((0, 0), (0, 0), (0, 8 - t)))
    conv0 = jnp.pad(state_conv, ((0, 0), (HALO - (B_CONV - 1), 0), (0, 0)))
    o, st_s = _gdn_chunked(pad_t(qkv), pad_t(ab.reshape(n, t, 2 * B_HEADS)), abt.reshape(n, 1, 2 * B_HEADS, 8),
                           conv0, state_ssm, w, t=8, valid=t, bb=4, o_dtype=F32)
    ys = _gated_out(xs.reshape(n * t, D_MODEL), o[:, :t].reshape(n * t, B_WIDTH), z, w["w_o"], tm=512)
    conv_s = jnp.concatenate([state_conv, qkv], 1)[:, -(B_CONV - 1):]
    return yp.reshape(b, s, D_MODEL), ys.reshape(n, t, D_MODEL), conv_p, st_p, conv_s, st_s


def kernel(x_prompt, x_sample, cache_latent, cache_krope, page_table, state_conv, state_ssm,
           a_norm, a_w_in, a_g_qa, a_w_uq, a_g_kv, a_w_uk, a_w_uv, a_g_q, a_g_k, a_w_o,
           b_norm, b_w_in, b_w_conv, b_a_log, b_dt_bias, b_g_o, b_w_o):
    s = x_prompt.shape[1]
    n, t, _ = x_sample.shape
    past = page_table.shape[1] * PAGE
    p_pad = -(-(s + t) // SLOT) * SLOT
    pos = jnp.concatenate([jnp.arange(s), past + jnp.arange(t), jnp.zeros((p_pad - s - t,), jnp.int32)]).astype(F32)
    tabs = _rope_tables(pos)
    tabs_p = tuple(tb[:s] for tb in tabs)
    tabs_s = tuple(jnp.tile(tb[s:s + t], (n, 1)) for tb in tabs)

    wa = _mla_weights(a_norm[0], a_w_in[0], a_g_qa[0], a_w_uq[0], a_g_kv[0], a_w_uk[0], a_w_uv[0], a_g_q[0],
                      a_g_k[0], a_w_o[0])
    wb = _gdn_weights(b_norm[0], b_w_in[0], b_w_conv[0], b_a_log[0], b_dt_bias[0], b_g_o[0], b_w_o[0])

    xp, xs, lat_p, kpe_p, lat_s, kpe_s = _mla_layer(x_prompt, x_sample, cache_latent, cache_krope, 0, page_table,
                                                    wa, tabs_p, tabs_s)
    xp, xs, conv_p, ssm_p, conv_s, ssm_s = _gdn_layer(xp, xs, state_conv[0], state_ssm[0], wb)
    return (xp, xs, lat_p[None], kpe_p[None], lat_s[None], kpe_s[None],
            conv_p[None], ssm_p[None], conv_s[None], ssm_s[None])
```

```python
import functools

import jax
import jax.numpy as jnp
from jax import lax
from jax.experimental import pallas as pl
from jax.experimental.pallas import tpu as pltpu

F32 = jnp.float32
BF16 = jnp.bfloat16
EPS = 1e-6

D_MODEL = 1024
PAGE = 128
A_HEADS = 8
A_NOPE = 64
A_ROPE = 32
A_QK = A_NOPE + A_ROPE
A_V = 64
A_Q_LORA = 384
A_KV_LORA = 256
A_WIDTH = A_HEADS * A_V
ROPE_THETA = 10000.0
SLOT = 128
A_QPAD = A_HEADS * SLOT
B_HEADS = 8
B_DK = 64
B_DV = 64
B_WIDTH = B_HEADS * B_DV
B_CONV = 4
B_CONV_CH = 2 * B_HEADS * B_DK + B_WIDTH
HALO = 8
BF16_ROWS = 16

VMEM_LIMIT = 56 * 1024 * 1024
FIXED_SHIFT_MAX = 30.0
BF16_NORM_SLACK = 1.02
PAGED_SLOTS = 3
PAGED_PARTS = 2


def _cparams(sem):
    return pltpu.CompilerParams(dimension_semantics=sem, vmem_limit_bytes=VMEM_LIMIT)


def _dot(a, b):
    return jnp.dot(a, b, preferred_element_type=F32)


def _dot_nt(a, b):
    return lax.dot_general(a, b, (((1,), (1,)), ((), ())), preferred_element_type=F32)


def _dot_tn(a, b):
    return lax.dot_general(a, b, (((0,), (0,)), ((), ())), preferred_element_type=F32)


def _dot_sel(sel_bf16, x, *, sel_first):
    x1 = x.astype(BF16)
    r1 = x - x1.astype(F32)
    x2 = r1.astype(BF16)
    x3 = (r1 - x2.astype(F32)).astype(BF16)
    if sel_first:
        return _dot(sel_bf16, x1) + _dot(sel_bf16, x2) + _dot(sel_bf16, x3)
    return _dot(x1, sel_bf16) + _dot(x2, sel_bf16) + _dot(x3, sel_bf16)


def _rms(x, g):
    return x * lax.rsqrt(jnp.mean(x * x, -1, keepdims=True) + EPS) * g


def _silu(x):
    return x * jax.nn.sigmoid(x)


def _softplus(x):
    return jnp.maximum(x, 0.0) + jnp.log(1.0 + jnp.exp(-jnp.abs(x)))


def _rope_table_kernel(pos_ref, inv_ref, cos_ref, sin_ref):
    ang = inv_ref[...] * pos_ref[...]
    cos_ref[...] = jnp.cos(ang)
    sin_ref[...] = jnp.sin(ang)


def _rope_tables(pos):
    half = A_ROPE // 2
    p = pos.shape[0]
    inv = (ROPE_THETA ** (-jnp.arange(half, dtype=F32) / half)).reshape(half, 1)
    cos_t, sin_t = pl.pallas_call(
        _rope_table_kernel,
        out_shape=(jax.ShapeDtypeStruct((half, p), F32),) * 2,
        name="rope_tables",
    )(pos.reshape(1, p), inv)
    cos = cos_t.T
    sin = sin_t.T
    one = jnp.ones((p, A_NOPE), F32)
    zn = jnp.zeros((p, A_NOPE), F32)
    zh = jnp.zeros((p, half), F32)
    zp = jnp.zeros((p, SLOT - A_QK), F32)
    tc = jnp.concatenate([one, cos, cos, zp], 1)
    ts1 = jnp.concatenate([zn, -sin, zh, zp], 1)
    ts2 = jnp.concatenate([zn, zh, sin, zp], 1)
    return tc, ts1, ts2


def _mla_proj_kernel(*refs, with_kv):
    (x_ref, tc_ref, ts1_ref, ts2_ref, gn_ref, win_ref, gqa_ref, wuq_ref, gkv_ref, gq_ref, ones_ref) = refs[:11]
    if with_kv:
        wuk_ref, gk_ref, wuv_ref = refs[11:14]
        q_out, c_out, kpe_out, z_out, k_out, v_out = refs[14:]
    else:
        q_out, c_out, kpe_out, z_out = refs[11:]
    x = x_ref[0]
    h = _rms(x, gn_ref[...])
    proj = _dot(h.astype(BF16), win_ref[...])
    o1 = A_Q_LORA
    o2 = o1 + A_KV_LORA
    o3 = o2 + A_WIDTH
    qa = _rms(proj[:, :o1], gqa_ref[...])
    c = _rms(proj[:, o1:o2], gkv_ref[...])
    z_out[0] = proj[:, o2:o3].astype(z_out.dtype)
    c_out[0] = c
    q = _dot(qa.astype(BF16), wuq_ref[...])
    tc = tc_ref[...]
    ts1 = ts1_ref[...]
    ts2 = ts2_ref[...]

    def rope(s):
        return s * tc + pltpu.roll(s, SLOT - A_ROPE // 2, 1) * ts1 + pltpu.roll(s, A_ROPE // 2, 1) * ts2

    ones = ones_ref[...]

    def head_norm(s, g):
        s2 = s * s
        s2h = s2.astype(BF16)
        ms = (_dot(s2h, ones) + _dot((s2 - s2h.astype(F32)).astype(BF16), ones)) * (1.0 / A_QK)
        return s * lax.rsqrt(ms + EPS) * g

    kslot = rope(proj[:, o3:o3 + SLOT])
    kpe_out[0] = kslot[:, A_NOPE:A_QK]
    gq = gq_ref[...]
    for hd in range(A_HEADS):
        sl = slice(hd * SLOT, (hd + 1) * SLOT)
        q_out[0, :, sl] = head_norm(rope(q[:, sl]), gq).astype(q_out.dtype)
    if with_kv:
        cb = c.astype(BF16)
        kn = _dot(cb, wuk_ref[...])
        v_out[0] = _dot(cb, wuv_ref[...]).astype(v_out.dtype)
        gk = gk_ref[...]
        for hd in range(A_HEADS):
            sl = slice(hd * SLOT, (hd + 1) * SLOT)
            k_out[0, :, sl] = head_norm(kn[:, sl] + kslot, gk).astype(k_out.dtype)


def _mla_project(x, tabs, w, *, with_kv, tm, q_dtype):
    b, s, _ = x.shape
    tm = min(tm, s)
    grid = (s // tm, b)
    full = lambda a: pl.BlockSpec(a.shape, lambda i, j: (0,) * a.ndim)
    row = lambda n: pl.BlockSpec((1, tm, n), lambda i, j: (j, i, 0))
    tab = pl.BlockSpec((tm, SLOT), lambda i, j: (i, 0))
    params = [w["a_norm"], w["w_in"], w["g_qa"], w["w_uq"], w["g_kv"], w["g_q"], jnp.ones((SLOT, SLOT), BF16)]
    out_shape = [
        jax.ShapeDtypeStruct((b, s, A_QPAD), q_dtype),
        jax.ShapeDtypeStruct((b, s, A_KV_LORA), F32),
        jax.ShapeDtypeStruct((b, s, A_ROPE), F32),
        jax.ShapeDtypeStruct((b, s, A_WIDTH), q_dtype),
    ]
    out_specs = [row(A_QPAD), row(A_KV_LORA), row(A_ROPE), row(A_WIDTH)]
    if with_kv:
        params += [w["w_uk"], w["g_k"], w["w_uv"]]
        out_shape += [jax.ShapeDtypeStruct((b, s, A_QPAD), BF16), jax.ShapeDtypeStruct((b, s, A_WIDTH), BF16)]
        out_specs += [row(A_QPAD), row(A_WIDTH)]
    return pl.pallas_call(
        functools.partial(_mla_proj_kernel, with_kv=with_kv),
        grid=grid,
        in_specs=[row(D_MODEL), tab, tab, tab] + [full(p) for p in params],
        out_specs=out_specs,
        out_shape=out_shape,
        compiler_params=_cparams(("parallel", "parallel")),
        name="mla_proj_kv" if with_kv else "mla_proj",
    )(x, *tabs, *params)


def _flash_kernel(qi_ref, ki_ref, dv_ref, fx_ref, q_ref, k_ref, v_ref, sh_ref, o_ref, m_ref, l_ref, acc_ref,
                  *, tq, tk):
    del qi_ref
    p = pl.program_id(1)
    ki = ki_ref[p]
    dv = dv_ref[p]
    fixed = fx_ref[0] == 1

    @pl.when(ki == 0)
    def _():
        m_ref[...] = jnp.full(m_ref.shape, -jnp.inf, F32)
        l_ref[...] = jnp.zeros(l_ref.shape, F32)
        acc_ref[...] = jnp.zeros(acc_ref.shape, F32)

    low = lax.broadcasted_iota(jnp.int32, (tq, SLOT), 1) < A_V

    def step(d, fixed_shift):
        diag = d is not None
        if diag:
            keep = (lax.broadcasted_iota(jnp.int32, (tq, tk), 1) + d * tk
                    <= lax.broadcasted_iota(jnp.int32, (tq, tk), 0))
        if fixed_shift:
            shift = sh_ref[:, :1]
        for j in range(A_HEADS // 2):
            pv = []
            al = []
            vpair = v_ref[0, :, j * SLOT:(j + 1) * SLOT]
            for e in range(2):
                hd = 2 * j + e
                sl = slice(hd * SLOT, (hd + 1) * SLOT)
                s = _dot_nt(q_ref[0, :, sl], k_ref[0, :, sl])
                if diag:
                    s = jnp.where(keep, s, -jnp.inf)
                if fixed_shift:
                    pr = jnp.exp(s - shift)
                    l_ref[hd] = l_ref[hd] + jnp.sum(pr, -1, keepdims=True)
                else:
                    m_prev = m_ref[hd]
                    m_new = jnp.maximum(m_prev, jnp.max(s, -1, keepdims=True))
                    alpha = jnp.exp(m_prev - m_new)
                    pr = jnp.exp(s - m_new[:, :1])
                    l_ref[hd] = alpha * l_ref[hd] + jnp.sum(pr, -1, keepdims=True)
                    m_ref[hd] = m_new
                    al.append(alpha)
                pv.append(_dot(pr.astype(BF16), vpair))
            sl = slice(j * SLOT, (j + 1) * SLOT)
            if fixed_shift:
                acc_ref[:, sl] = acc_ref[:, sl] + jnp.where(low, pv[0], pv[1])
            else:
                acc_ref[:, sl] = jnp.where(low, al[0], al[1]) * acc_ref[:, sl] + jnp.where(low, pv[0], pv[1])

    def finish():
        for j in range(A_HEADS // 2):
            sl = slice(j * SLOT, (j + 1) * SLOT)
            linv = jnp.where(low, 1.0 / l_ref[2 * j], 1.0 / l_ref[2 * j + 1])
            o_ref[0, :, sl] = (acc_ref[:, sl] * linv).astype(o_ref.dtype)

    n_diag = tq // tk
    for fs in (True, False):
        mode = fixed if fs else jnp.logical_not(fixed)

        @pl.when(jnp.logical_and(mode, dv < 0))
        def _():
            step(None, fs)

        for d in range(n_diag):
            @pl.when(jnp.logical_and(mode, dv == d))
            def _():
                step(d, fs)
                if d == n_diag - 1:
                    finish()


def _flash_attention(q, k, v, bound, *, tq, tk):
    b, s, _ = q.shape
    tq = min(tq, s)
    tk = min(tk, tq)
    assert s % tq == 0 and tq % tk == 0
    nq = s // tq
    r = tq // tk
    pairs = [(i, j, j - r * i if j >= r * i else -1) for i in range(nq) for j in range(r * (i + 1))]
    qi = jnp.asarray([pr[0] for pr in pairs], jnp.int32)
    ki = jnp.asarray([pr[1] for pr in pairs], jnp.int32)
    dv = jnp.asarray([pr[2] for pr in pairs], jnp.int32)
    fx = (bound <= FIXED_SHIFT_MAX).astype(jnp.int32).reshape(1)
    sh = jnp.full((1, SLOT), bound, F32)
    grid_spec = pltpu.PrefetchScalarGridSpec(
        num_scalar_prefetch=4,
        grid=(b, len(pairs)),
        in_specs=[
            pl.BlockSpec((1, tq, A_QPAD), lambda bi, p, qt, kt, dt, fx: (bi, qt[p], 0)),
            pl.BlockSpec((1, tk, A_QPAD), lambda bi, p, qt, kt, dt, fx: (bi, kt[p], 0)),
            pl.BlockSpec((1, tk, A_WIDTH), lambda bi, p, qt, kt, dt, fx: (bi, kt[p], 0)),
            pl.BlockSpec((1, SLOT), lambda bi, p, qt, kt, dt, fx: (0, 0)),
        ],
        out_specs=pl.BlockSpec((1, tq, A_WIDTH), lambda bi, p, qt, kt, dt, fx: (bi, qt[p], 0)),
        scratch_shapes=[
            pltpu.VMEM((A_HEADS, tq, SLOT), F32),
            pltpu.VMEM((A_HEADS, tq, SLOT), F32),
            pltpu.VMEM((tq, A_WIDTH), F32),
        ],
    )
    return pl.pallas_call(
        functools.partial(_flash_kernel, tq=tq, tk=tk),
        grid_spec=grid_spec,
        out_shape=jax.ShapeDtypeStruct((b, s, A_WIDTH), BF16),
        compiler_params=_cparams(("parallel", "arbitrary")),
        name="mla_flash",
    )(qi, ki, dv, fx, q, k, v, sh)


def _paged_kernel(pt_ref, q_ref, cn_ref, kn_ref, gk_ref, wukt_ref, wuv_ref, lat_hbm, kr_hbm, o_ref,
                  wq_ref, qr_ref, m_ref, l_ref, acc_ref, lat_buf, kr_buf, sem, *, la, ppb, n_blocks):
    smp = pl.program_id(0)
    blk = pl.program_id(1)
    rows = A_HEADS * 8
    hn = A_HEADS * A_NOPE

    step = smp * n_blocks + blk
    n_steps = pl.num_programs(0) * n_blocks
    ahead = PAGED_SLOTS - 1

    def page_copies(g):
        s_ = g // n_blocks
        b_ = g % n_blocks
        slot_ = g % PAGED_SLOTS
        out = []
        for i in range(ppb):
            page = pt_ref[s_, b_ * ppb + i]
            out.append(pltpu.make_async_copy(lat_hbm.at[la, page], lat_buf.at[slot_, pl.ds(i * PAGE, PAGE)],
                                             sem.at[slot_]))
            out.append(pltpu.make_async_copy(kr_hbm.at[la, page], kr_buf.at[slot_, :, pl.ds(i * PAGE, PAGE)],
                                             sem.at[slot_]))
        return out

    for d in range(ahead):
        @pl.when(jnp.logical_and(step == 0, d < n_steps))
        def _():
            for i, cp in enumerate(page_copies(step + d)):
                cp.start(priority=i % 2)

    @pl.when(step + ahead < n_steps)
    def _():
        for i, cp in enumerate(page_copies(step + ahead)):
            cp.start(priority=i % 2)

    for cp in page_copies(step):
        cp.wait()
    slot = step % PAGED_SLOTS

    @pl.when(blk == 0)
    def _():
        m_ref[...] = jnp.full(m_ref.shape, -jnp.inf, F32)
        l_ref[...] = jnp.zeros(l_ref.shape, F32)
        acc_ref[...] = jnp.zeros(acc_ref.shape, F32)
        gk = gk_ref[...]
        for hd in range(A_HEADS):
            qs = q_ref[0, :, hd * SLOT:(hd + 1) * SLOT] * gk
            wq_ref[hd * A_NOPE:(hd + 1) * A_NOPE, :] = wukt_ref[hd]
            wq_ref[hn + hd * 8:hn + (hd + 1) * 8, :] = _dot(qs[:, :A_NOPE].astype(BF16), wukt_ref[hd]).astype(BF16)
            qr_ref[hd * 8:(hd + 1) * 8, :] = qs[:, A_NOPE:A_QK].astype(BF16)

    def scores(cb, krt):
        nk = cb.shape[0]
        res = _dot_nt(wq_ref[...], cb)
        ssq = []
        for hd in range(A_HEADS):
            kh = res[hd * A_NOPE:(hd + 1) * A_NOPE]
            ssq.append(jnp.broadcast_to(jnp.sum(kh * kh, 0, keepdims=True), (8, nk)))
        ssq = jnp.concatenate(ssq, 0) + jnp.sum(krt * krt, 0, keepdims=True)
        s = res[hn:] + _dot(qr_ref[...], krt.astype(BF16))
        return s * lax.rsqrt(ssq * (1.0 / A_QK) + EPS)

    def block(cbs, krts, new):
        sts = [scores(cb, krt) for cb, krt in zip(cbs, krts)]
        if new:
            nk = cbs[0].shape[0]
            tok = lax.broadcasted_iota(jnp.int32, (rows, nk), 0) & 7
            key = lax.broadcasted_iota(jnp.int32, (rows, nk), 1)
            sts = [jnp.where(key <= tok, st, -jnp.inf) for st in sts]
        ms = [jnp.max(st, -1, keepdims=True) for st in sts]
        prs = [jnp.exp(st - m) for st, m in zip(sts, ms)]
        ls = [jnp.sum(pr, -1, keepdims=True) for pr in prs]
        pvs = [_dot(pr.astype(BF16), cb) for pr, cb in zip(prs, cbs)]
        m_prev = m_ref[...]
        m_new = m_prev
        for m in ms:
            m_new = jnp.maximum(m_new, m)
        alpha = jnp.exp(m_prev - m_new)
        l_new = alpha * l_ref[...]
        acc = jnp.concatenate([alpha, alpha], 1) * acc_ref[...]
        for m, l, pv in zip(ms, ls, pvs):
            wgt = jnp.exp(m - m_new)
            l_new = l_new + wgt * l
            acc = acc + jnp.concatenate([wgt, wgt], 1) * pv
        m_ref[...] = m_new
        l_ref[...] = l_new
        acc_ref[...] = acc

    sub = max(ppb // PAGED_PARTS, 1) * PAGE
    block([lat_buf[slot, i:i + sub].astype(BF16) for i in range(0, ppb * PAGE, sub)],
          [kr_buf[slot, :, i:i + sub] for i in range(0, ppb * PAGE, sub)], False)

    @pl.when(blk == n_blocks - 1)
    def _():
        block([cn_ref[0].astype(BF16)], [kn_ref[0]], True)
        o_lat = (acc_ref[...] * (1.0 / l_ref[...][:, :1])).astype(BF16)
        for hd in range(A_HEADS):
            o_ref[0, :, hd * A_V:(hd + 1) * A_V] = _dot(o_lat[hd * 8:(hd + 1) * 8], wuv_ref[hd])


def _paged_attention(q, c_new, kpe_new_t, cache_latent, cache_krope_t, la, page_table, w, *, ppb):
    n = q.shape[0]
    n_pages = page_table.shape[1]
    ppb = min(ppb, n_pages)
    n_blocks = n_pages // ppb
    full = lambda a: pl.BlockSpec(a.shape, lambda i, j, pt: (0,) * a.ndim)
    rows = A_HEADS * 8
    params = [w["g_k"], w["w_ukt"], w["w_uv_h"]]

    hbm = pl.BlockSpec(memory_space=pl.ANY)
    grid_spec = pltpu.PrefetchScalarGridSpec(
        num_scalar_prefetch=1,
        grid=(n, n_blocks),
        in_specs=[
            pl.BlockSpec((1, 8, A_QPAD), lambda s, j, pt: (s, 0, 0)),
            pl.BlockSpec((1, PAGE, A_KV_LORA), lambda s, j, pt: (s, 0, 0)),
            pl.BlockSpec((1, A_ROPE, PAGE), lambda s, j, pt: (s, 0, 0)),
        ] + [full(p) for p in params] + [hbm, hbm],
        out_specs=pl.BlockSpec((1, 8, A_WIDTH), lambda s, j, pt: (s, 0, 0)),
        scratch_shapes=[
            pltpu.VMEM((A_HEADS * A_NOPE + rows, A_KV_LORA), BF16),
            pltpu.VMEM((rows, A_ROPE), BF16),
            pltpu.VMEM((rows, SLOT), F32),
            pltpu.VMEM((rows, SLOT), F32),
            pltpu.VMEM((rows, A_KV_LORA), F32),
            pltpu.VMEM((PAGED_SLOTS, ppb * PAGE, A_KV_LORA), F32),
            pltpu.VMEM((PAGED_SLOTS, A_ROPE, ppb * PAGE), F32),
            pltpu.SemaphoreType.DMA((PAGED_SLOTS,)),
        ],
    )
    return pl.pallas_call(
        functools.partial(_paged_kernel, la=la, ppb=ppb, n_blocks=n_blocks),
        grid_spec=grid_spec,
        out_shape=jax.ShapeDtypeStruct((n, 8, A_WIDTH), F32),
        compiler_params=_cparams(("arbitrary", "arbitrary")),
        name="mla_paged",
    )(page_table, q, c_new, kpe_new_t, *params, cache_latent, cache_krope_t)


def _gated_out_kernel(x_ref, o_ref, z_ref, w_ref, y_ref):
    gated = o_ref[...].astype(F32) * _silu(z_ref[...].astype(F32))
    y_ref[...] = x_ref[...] + _dot(gated.astype(BF16), w_ref[...])


def _gated_out(x, o, z, w_o, *, tm):
    r = x.shape[0]
    tm = min(tm, r)
    row = lambda n: pl.BlockSpec((tm, n), lambda i: (i, 0))
    return pl.pallas_call(
        _gated_out_kernel,
        grid=(r // tm,),
        in_specs=[row(D_MODEL), row(o.shape[1]), row(z.shape[1]), pl.BlockSpec(w_o.shape, lambda i: (0, 0))],
        out_specs=row(D_MODEL),
        out_shape=jax.ShapeDtypeStruct((r, D_MODEL), F32),
        compiler_params=_cparams(("parallel",)),
        name="gated_out",
    )(x, o, z, w_o)


def _gdn_proj_kernel(x_ref, gn_ref, w_ref, wabt_ref, qkv_out, z_out, ab_out, abt_out):
    hb = _rms(x_ref[...], gn_ref[...]).astype(BF16)
    proj = _dot(hb, w_ref[...])
    o1 = B_CONV_CH
    o2 = o1 + B_WIDTH
    qkv_out[...] = proj[:, :o1]
    z_out[...] = proj[:, o1:o2].astype(z_out.dtype)
    ab_out[...] = proj[:, o2:o2 + 2 * B_HEADS]
    abt_out[...] = _dot_nt(wabt_ref[...], hb)


def _gdn_project(x, w, *, tm, z_dtype):
    r = x.shape[0]
    tm = min(tm, r)
    row = lambda n: pl.BlockSpec((tm, n), lambda i: (i, 0))
    full = lambda a: pl.BlockSpec(a.shape, lambda i: (0,) * a.ndim)
    params = [w["b_norm"], w["w_in"], w["w_abt"]]
    return pl.pallas_call(
        _gdn_proj_kernel,
        grid=(r // tm,),
        in_specs=[row(D_MODEL)] + [full(p) for p in params],
        out_specs=[row(B_CONV_CH), row(B_WIDTH), row(2 * B_HEADS), pl.BlockSpec((2 * B_HEADS, tm), lambda i: (0, i))],
        out_shape=[
            jax.ShapeDtypeStruct((r, B_CONV_CH), F32),
            jax.ShapeDtypeStruct((r, B_WIDTH), z_dtype),
            jax.ShapeDtypeStruct((r, 2 * B_HEADS), F32),
            jax.ShapeDtypeStruct((2 * B_HEADS, r), F32),
        ],
        compiler_params=_cparams(("parallel",)),
        name="gdn_proj",
    )(x, *params)


def _gdn_chunk_kernel(qkv_ref, ab_ref, abt_ref, conv0_ref, st0_ref, wc_ref, alr_ref, dtr_ref, alc_ref, dtc_ref,
                      go_ref, exp_ref, tri_ref, triu_ref, off_ref, avg_ref, o_ref, st_out, ext_ref, feat_ref, gcx_ref,
                      bx_ref, st_ref, lm_ref, inv_ref, t1_ref, in_ref, *, t, valid, bb):
    si = pl.program_id(1)
    hk = B_HEADS * B_DK
    units = [(bi, hd) for bi in range(bb) for hd in range(B_HEADS)]
    qsl = lambda hd: slice(hd * B_DK, (hd + 1) * B_DK)
    ksl = lambda hd: slice(hk + hd * B_DK, hk + (hd + 1) * B_DK)
    vsl = lambda hd: slice(2 * hk + hd * B_DV, 2 * hk + (hd + 1) * B_DV)

    @pl.when(si == 0)
    def _():
        ext_ref[:, 0:HALO] = conv0_ref[...]
        st_ref[...] = st0_ref[...]

    gcts = []
    for bi in range(bb):
        ext_ref[bi, HALO:HALO + t] = qkv_ref[bi]
        conv = ext_ref[bi, HALO - 3:HALO - 3 + t] * wc_ref[0:1]
        for j in range(1, B_CONV):
            conv = conv + ext_ref[bi, HALO - 3 + j:HALO - 3 + j + t] * wc_ref[j:j + 1]
        feat_ref[bi] = _silu(conv)
        ext_ref[bi, 0:HALO] = ext_ref[bi, t:t + HALO]

        ab = ab_ref[bi]
        g = -jnp.exp(alr_ref[...]) * _softplus(ab[:, :B_HEADS] + dtr_ref[...])
        beta = jax.nn.sigmoid(ab[:, B_HEADS:])
        if valid < t:
            live = lax.broadcasted_iota(jnp.int32, (t, B_HEADS), 0) < valid
            g = jnp.where(live, g, 0.0)
            beta = jnp.where(live, beta, 0.0)
        bx_ref[bi] = _dot_sel(exp_ref[...], beta, sel_first=False)
        gcx_ref[bi] = _dot_sel(tri_ref[...], _dot_sel(exp_ref[...], g, sel_first=False), sel_first=True)
        gt = -jnp.exp(alc_ref[...]) * _softplus(abt_ref[bi, 0][:B_HEADS] + dtc_ref[...])
        if valid < t:
            gt = jnp.where(lax.broadcasted_iota(jnp.int32, (B_HEADS, t), 1) < valid, gt, 0.0)
        gcts.append(_dot_sel(triu_ref[...], gt, sel_first=False))

    ii = lax.broadcasted_iota(jnp.int32, (t, t), 0)
    jj = lax.broadcasted_iota(jnp.int32, (t, t), 1)
    incl = ii >= jj
    strict = ii > jj
    eye = (ii == jj).astype(F32)
    pair = (ii >> 1) == (jj >> 1)

    def cols(x):
        return x[:, :t] if t <= B_DK else jnp.tile(x, (1, t // B_DK))

    for u, (bi, hd) in enumerate(units):
        qh = feat_ref[bi, :, qsl(hd)]
        kh = feat_ref[bi, :, ksl(hd)]
        qh = qh * lax.rsqrt(jnp.sum(qh * qh, -1, keepdims=True) + EPS) * (B_DK ** -0.5)
        kh = kh * lax.rsqrt(jnp.sum(kh * kh, -1, keepdims=True) + EPS)
        feat_ref[bi, :, qsl(hd)] = qh
        feat_ref[bi, :, ksl(hd)] = kh
        decay = jnp.exp(jnp.where(incl, cols(gcx_ref[bi, :, qsl(hd)]) - gcts[bi][hd:hd + 1, :], -jnp.inf))
        khb = kh.astype(BF16)
        lmat = jnp.where(strict, _dot_nt((kh * bx_ref[bi, :, qsl(hd)]).astype(BF16), khb) * decay, 0.0)
        lm_ref[u] = lmat.astype(BF16)
        in_ref[u] = (_dot_nt(qh.astype(BF16), khb) * decay).astype(BF16)
        inv_ref[u] = (eye - jnp.where(pair, lmat, 0.0)).astype(BF16)

    for lv in range(off_ref.shape[0]):
        bsz = 2 << lv
        if bsz % BF16_ROWS == 0:
            rngs = [((2 * m + 1) * bsz, (2 * m + 2) * bsz) for m in range(t // (2 * bsz))]
            rows_of = lambda ref, i: jnp.concatenate([ref[i, a:b, :] for a, b in rngs], 0)
            zero = jnp.zeros((bsz, t), BF16)
            for u in range(len(units)):
                half = _dot(rows_of(lm_ref, u) * rows_of(off_ref, lv), inv_ref[u]).astype(BF16)
                pieces = []
                for m in range(len(rngs)):
                    pieces += [zero, half[m * bsz:(m + 1) * bsz]]
                t1_ref[u] = jnp.concatenate(pieces, 0)
            for u in range(len(units)):
                upd = _dot(rows_of(inv_ref, u), t1_ref[u]).astype(BF16)
                for m, (a, b) in enumerate(rngs):
                    inv_ref[u, a:b, :] = inv_ref[u, a:b, :] - upd[m * bsz:(m + 1) * bsz]
        else:
            for u in range(len(units)):
                t1_ref[u] = _dot(lm_ref[u] * off_ref[lv], inv_ref[u]).astype(BF16)
            for u in range(len(units)):
                inv_ref[u] = inv_ref[u] - _dot(inv_ref[u], t1_ref[u]).astype(BF16)

    go = go_ref[...]
    us, ws, egs = [], [], []
    for u, (bi, hd) in enumerate(units):
        invb = inv_ref[u]
        bc = bx_ref[bi, :, qsl(hd)]
        eg = jnp.exp(gcx_ref[bi, :, qsl(hd)])
        kb = feat_ref[bi, :, ksl(hd)] * bc
        us.append(_dot(invb, (feat_ref[bi, :, vsl(hd)] * bc).astype(BF16)))
        ws.append(_dot(invb, (kb * eg).astype(BF16)).astype(BF16))
        egs.append(eg)
    vns = []
    for u, (bi, hd) in enumerate(units):
        vns.append((us[u] - _dot(ws[u], st_ref[bi, hd].astype(BF16))).astype(BF16))
    for u, (bi, hd) in enumerate(units):
        st = st_ref[bi, hd]
        o = _dot((feat_ref[bi, :, qsl(hd)] * egs[u]).astype(BF16), st.astype(BF16)) + _dot(in_ref[u], vns[u])
        gcc = gcx_ref[bi, :, qsl(hd)]
        glast = gcc[t - 1:t, :]
        kdec = feat_ref[bi, :, ksl(hd)] * jnp.exp(glast - gcc)
        st_ref[bi, hd] = st * jnp.exp(glast) + _dot_tn(kdec.astype(BF16), vns[u])
        if t >= B_DV:
            o2 = o * o
            o2h = o2.astype(BF16)
            ms = _dot(o2h, avg_ref[...]) + _dot((o2 - o2h.astype(F32)).astype(BF16), avg_ref[...])
        else:
            ms = jnp.mean(o * o, -1, keepdims=True)
        o_ref[bi, :, qsl(hd)] = (o * lax.rsqrt(ms + EPS) * go).astype(o_ref.dtype)

    @pl.when(si == pl.num_programs(1) - 1)
    def _():
        st_out[...] = st_ref[...]


def _gdn_chunked(qkv, ab, abt, conv0, st0, w, *, t, valid, bb, o_dtype):
    b, s, _ = qkv.shape
    t = min(t, s)
    full = lambda a: pl.BlockSpec(a.shape, lambda i, j: (0,) * a.ndim)
    expand = jnp.repeat(jnp.eye(B_HEADS, dtype=BF16), B_DV, axis=1)
    pos = jnp.arange(t)
    tri = (pos[:, None] >= pos[None, :]).astype(BF16)
    triu = (pos[:, None] <= pos[None, :]).astype(BF16)
    blk = lambda lg: pos >> lg
    off = jnp.stack([((blk(lg)[:, None] - blk(lg)[None, :] == 1) & ((blk(lg)[:, None] & 1) == 1)).astype(BF16)
                     for lg in range(1, t.bit_length() - 1)])
    avg = jnp.full((B_DV, B_DV), 1.0 / B_DV, BF16)
    params = [w["w_conv"], w["a_log_r"], w["dt_r"], w["a_log_c"], w["dt_c"], w["g_o"], expand, tri, triu, off, avg]
    bb = min(bb, b)
    assert b % bb == 0 and s % t == 0
    return pl.pallas_call(
        functools.partial(_gdn_chunk_kernel, t=t, valid=valid, bb=bb),
        grid=(b // bb, s // t),
        in_specs=[
            pl.BlockSpec((bb, t, B_CONV_CH), lambda i, j: (i, j, 0)),
            pl.BlockSpec((bb, t, 2 * B_HEADS), lambda i, j: (i, j, 0)),
            pl.BlockSpec((bb, 1, 2 * B_HEADS, t), lambda i, j: (i, j, 0, 0)),
            pl.BlockSpec((bb, HALO, B_CONV_CH), lambda i, j: (i, 0, 0)),
            pl.BlockSpec((bb, B_HEADS, B_DK, B_DV), lambda i, j: (i, 0, 0, 0)),
        ] + [full(p) for p in params],
        out_specs=[
            pl.BlockSpec((bb, t, B_WIDTH), lambda i, j: (i, j, 0)),
            pl.BlockSpec((bb, B_HEADS, B_DK, B_DV), lambda i, j: (i, 0, 0, 0)),
        ],
        out_shape=[
            jax.ShapeDtypeStruct((b, s, B_WIDTH), o_dtype),
            jax.ShapeDtypeStruct((b, B_HEADS, B_DK, B_DV), F32),
        ],
        scratch_shapes=[
            pltpu.VMEM((bb, t + HALO, B_CONV_CH), F32),
            pltpu.VMEM((bb, t, B_CONV_CH), F32),
            pltpu.VMEM((bb, t, B_WIDTH), F32),
            pltpu.VMEM((bb, t, B_WIDTH), F32),
            pltpu.VMEM((bb, B_HEADS, B_DK, B_DV), F32),
            pltpu.VMEM((bb * B_HEADS, t, t), BF16),
            pltpu.VMEM((bb * B_HEADS, t, t), BF16),
            pltpu.VMEM((bb * B_HEADS, t, t), BF16),
            pltpu.VMEM((bb * B_HEADS, t, t), BF16),
        ],
        compiler_params=_cparams(("parallel", "arbitrary")),
        name="gdn_chunk",
    )(qkv, ab, abt, conv0, st0, *params)


def _pad_heads(wm, n_in):
    d = wm.shape[-1]
    return jnp.pad(wm, ((0, 0), (0, 0), (0, SLOT - d))).reshape(n_in, A_HEADS * SLOT)


def _mla_weights(a_norm, a_w_in, a_g_qa, a_w_uq, a_g_kv, a_w_uk, a_w_uv, a_g_q, a_g_k, a_w_o):
    o1 = A_Q_LORA
    o2 = o1 + A_KV_LORA
    o3 = o2 + A_ROPE
    kslot = jnp.pad(a_w_in[:, o2:o3], ((0, 0), (A_NOPE, SLOT - A_QK)))
    w_in = jnp.concatenate([a_w_in[:, :o2], a_w_in[:, o3:], kslot], 1).astype(BF16)
    pad_gain = lambda g: jnp.pad(g, (0, SLOT - A_QK)).reshape(1, SLOT)
    return {
        "a_norm": a_norm.reshape(1, D_MODEL),
        "w_in": w_in,
        "g_qa": a_g_qa.reshape(1, A_Q_LORA),
        "w_uq": _pad_heads(a_w_uq.reshape(A_Q_LORA, A_HEADS, A_QK), A_Q_LORA).astype(BF16),
        "g_kv": a_g_kv.reshape(1, A_KV_LORA),
        "g_q": pad_gain(a_g_q * (A_QK ** -0.5)),
        "g_k": pad_gain(a_g_k),
        "w_uk": _pad_heads(a_w_uk, A_KV_LORA).astype(BF16),
        "w_ukt": jnp.transpose(a_w_uk, (1, 2, 0)).astype(BF16),
        "w_uv": a_w_uv.reshape(A_KV_LORA, A_WIDTH).astype(BF16),
        "w_uv_h": jnp.transpose(a_w_uv, (1, 0, 2)).astype(BF16),
        "w_o": a_w_o.astype(BF16),
    }


def _gdn_weights(b_norm, b_w_in, b_w_conv, b_a_log, b_dt_bias, b_g_o, b_w_o):
    o2 = B_CONV_CH + B_WIDTH
    w_in = jnp.pad(b_w_in, ((0, 0), (0, SLOT - 2 * B_HEADS))).astype(BF16)
    return {
        "b_norm": b_norm.reshape(1, D_MODEL),
        "w_in": w_in,
        "w_abt": b_w_in[:, o2:].T.astype(BF16),
        "w_conv": b_w_conv,
        "a_log_r": b_a_log.reshape(1, B_HEADS),
        "dt_r": b_dt_bias.reshape(1, B_HEADS),
        "a_log_c": b_a_log.reshape(B_HEADS, 1),
        "dt_c": b_dt_bias.reshape(B_HEADS, 1),
        "g_o": b_g_o.reshape(1, B_DV),
        "w_o": b_w_o.astype(BF16),
    }


def _mla_layer(xp, xs, cache_latent, cache_krope, la, page_table, w, tabs_p, tabs_s):
    b, s, _ = xp.shape
    n, t, _ = xs.shape
    q, c_p, kpe_p, z, k, v = _mla_project(xp, tabs_p, w, with_kv=True, tm=256, q_dtype=BF16)
    bound = A_QK * BF16_NORM_SLACK * jnp.max(jnp.abs(w["g_q"])) * jnp.max(jnp.abs(w["g_k"]))
    o = _flash_attention(q, k, v, bound, tq=512, tk=512)
    yp = _gated_out(xp.reshape(b * s, D_MODEL), o.reshape(b * s, A_WIDTH), z.reshape(b * s, A_WIDTH), w["w_o"], tm=512)

    q, c_s, kpe_s, z = _mla_project(xs.reshape(1, n * t, D_MODEL), tabs_s, w, with_kv=False, tm=256, q_dtype=F32)
    c_s = c_s.reshape(n, t, A_KV_LORA)
    kpe_s = kpe_s.reshape(n, t, A_ROPE)
    q8 = jnp.pad(q.reshape(n, t, A_QPAD), ((0, 0), (0, 8 - t), (0, 0)))
    c_new = jnp.pad(c_s, ((0, 0), (0, PAGE - t), (0, 0)))
    kpe_new_t = jnp.pad(jnp.swapaxes(kpe_s, 1, 2), ((0, 0), (0, 0), (0, PAGE - t)))
    o = _paged_attention(q8, c_new, kpe_new_t, cache_latent, jnp.swapaxes(cache_krope, 2, 3), la, page_table, w,
                         ppb=32)
    ys = _gated_out(xs.reshape(n * t, D_MODEL), o[:, :t].reshape(n * t, A_WIDTH), z.reshape(n * t, A_WIDTH),
                    w["w_o"], tm=512)
    return yp.reshape(b, s, D_MODEL), ys.reshape(n, t, D_MODEL), c_p, kpe_p, c_s, kpe_s


def _gdn_layer(xp, xs, state_conv, state_ssm, w):
    b, s, _ = xp.shape
    n, t, _ = xs.shape
    tc = min(256, s)
    qkv, z, ab, abt = _gdn_project(xp.reshape(b * s, D_MODEL), w, tm=256, z_dtype=BF16)
    qkv = qkv.reshape(b, s, B_CONV_CH)
    abt = abt.reshape(2 * B_HEADS, b, s // tc, tc).transpose(1, 2, 0, 3)
    o, st_p = _gdn_chunked(qkv, ab.reshape(b, s, 2 * B_HEADS), abt,
                           jnp.zeros((b, HALO, B_CONV_CH), F32), jnp.zeros((b, B_HEADS, B_DK, B_DV), F32),
                           w, t=tc, valid=tc, bb=1, o_dtype=BF16)
    yp = _gated_out(xp.reshape(b * s, D_MODEL), o.reshape(b * s, B_WIDTH), z, w["w_o"], tm=512)
    conv_p = qkv[:, s - (B_CONV - 1):]

    qkv, z, ab, abt = _gdn_project(xs.reshape(n * t, D_MODEL), w, tm=256, z_dtype=F32)
    qkv = qkv.reshape(n, t, B_CONV_CH)
    pad_t = lambda a: jnp.pad(a, ((0, 0), (0, 8 - t), (0, 0)))
    abt = jnp.pad(abt.reshape(2 * B_HEADS, n, t).transpose(1, 0, 2), ((0, 0), (0, 0), (0, 8 - t)))
    conv0 = jnp.pad(state_conv, ((0, 0), (HALO - (B_CONV - 1), 0), (0, 0)))
    o, st_s = _gdn_chunked(pad_t(qkv), pad_t(ab.reshape(n, t, 2 * B_HEADS)), abt.reshape(n, 1, 2 * B_HEADS, 8),
                           conv0, state_ssm, w, t=8, valid=t, bb=4, o_dtype=F32)
    ys = _gated_out(xs.reshape(n * t, D_MODEL), o[:, :t].reshape(n * t, B_WIDTH), z, w["w_o"], tm=512)
    conv_s = jnp.concatenate([state_conv, qkv], 1)[:, -(B_CONV - 1):]
    return yp.reshape(b, s, D_MODEL), ys.reshape(n, t, D_MODEL), conv_p, st_p, conv_s, st_s


def kernel(x_prompt, x_sample, cache_latent, cache_krope, page_table, state_conv, state_ssm,
           a_norm, a_w_in, a_g_qa, a_w_uq, a_g_kv, a_w_uk, a_w_uv, a_g_q, a_g_k, a_w_o,
           b_norm, b_w_in, b_w_conv, b_a_log, b_dt_bias, b_g_o, b_w_o):
    s = x_prompt.shape[1]
    n, t, _ = x_sample.shape
    past = page_table.shape[1] * PAGE
    p_pad = -(-(s + t) // SLOT) * SLOT
    pos = jnp.concatenate([jnp.arange(s), past + jnp.arange(t), jnp.zeros((p_pad - s - t,), jnp.int32)]).astype(F32)
    tabs = _rope_tables(pos)
    tabs_p = tuple(tb[:s] for tb in tabs)
    tabs_s = tuple(jnp.tile(tb[s:s + t], (n, 1)) for tb in tabs)

    wa = _mla_weights(a_norm[0], a_w_in[0], a_g_qa[0], a_w_uq[0], a_g_kv[0], a_w_uk[0], a_w_uv[0], a_g_q[0],
                      a_g_k[0], a_w_o[0])
    wb = _gdn_weights(b_norm[0], b_w_in[0], b_w_conv[0], b_a_log[0], b_dt_bias[0], b_g_o[0], b_w_o[0])

    xp, xs, lat_p, kpe_p, lat_s, kpe_s = _mla_layer(x_prompt, x_sample, cache_latent, cache_krope, 0, page_table,
                                                    wa, tabs_p, tabs_s)
    xp, xs, conv_p, ssm_p, conv_s, ssm_s = _gdn_layer(xp, xs, state_conv[0], state_ssm[0], wb)
    return (xp, xs, lat_p[None], kpe_p[None], lat_s[None], kpe_s[None],
            conv_p[None], ssm_p[None], conv_s[None], ssm_s[None])
```
